```python
import jax, jax.numpy as jnp
from jax import lax
import numpy as np

D_MODEL = 1024
BATCH = 8
SEQ = 4096
DEPTH = 4

D_MIX = D_MODEL
HEAD_DIM = 64
N_ATT_HEADS = 8
D_ATT = N_ATT_HEADS * HEAD_DIM
N_GM_GROUPS = 8
GM_GROUP_DIM = 64
D_GM = N_GM_GROUPS * GM_GROUP_DIM
BLOCK = 128
CHUNK = 128
D_IN = 3 * D_ATT + N_ATT_HEADS + 2 * D_GM
D_FF = ((8 * D_MODEL // 3 + 255) // 256) * 256
PLE_DIM = 256
EPS = 1e-6
NEG_INF = -1e30

kernel_name = "hybrid_fox_gmlp_sandwich_ple"


def rmsnorm(x, gain=None):
    xf = x.astype(jnp.float32)
    y = xf * lax.rsqrt(jnp.mean(xf * xf, axis=-1, keepdims=True) + EPS)
    if gain is not None:
        y = y * gain.astype(jnp.float32)
    return y.astype(x.dtype)


def fox_attention(q, k, v, log_f):
    S = q.shape[1]
    dh = q.shape[-1]
    scale = dh ** -0.5
    c = jnp.cumsum(log_f.astype(jnp.float32), axis=1).transpose(0, 2, 1)
    q_idx = jnp.arange(BLOCK)
    outs = []
    for i in range(S // BLOCK):
        lo, hi = i * BLOCK, (i + 1) * BLOCK
        s = jnp.einsum('bqhd,bkhd->bhqk', q[:, lo:hi], k[:, :hi],
                       preferred_element_type=jnp.float32) * scale
        s = s + c[:, :, lo:hi, None] - c[:, :, None, :hi]
        causal = (lo + q_idx)[:, None] >= jnp.arange(hi)[None, :]
        s = jnp.where(causal, s, NEG_INF)
        w = jax.nn.softmax(s, axis=-1).astype(v.dtype)
        outs.append(jnp.einsum('bhqk,bkhd->bqhd', w, v[:, :hi]))
    return jnp.concatenate(outs, axis=1)


def chunked_spatial_gating(g, w_s, b_s, v_gain):
    B, S, _ = g.shape
    u, vv = jnp.split(jax.nn.gelu(g), 2, axis=-1)
    shp = (B, S // CHUNK, CHUNK, N_GM_GROUPS, GM_GROUP_DIM)
    vf = vv.reshape(shp).astype(jnp.float32)
    mu = jnp.mean(vf, axis=-1, keepdims=True)
    var = jnp.mean(jnp.square(vf - mu), axis=-1, keepdims=True)
    vn = ((vf - mu) * lax.rsqrt(var + EPS) * v_gain.reshape(N_GM_GROUPS, GM_GROUP_DIM).astype(jnp.float32)).astype(g.dtype)
    tril = jnp.tril(jnp.ones((CHUNK, CHUNK), dtype=w_s.dtype))
    w = w_s * tril[None]
    mixed = jnp.einsum('gts,bnsgd->bntgd', w, vn) + b_s.T[:, :, None]
    return (u.reshape(shp) * mixed).reshape(B, S, D_GM)


def setup_inputs(seed: int = 0) -> dict:
    key = jax.random.key(seed)
    ks = jax.random.split(key, 20)
    L = DEPTH
    f32 = jnp.float32
    def nrm(k, shape, scale):
        return jax.random.normal(k, shape, f32) * scale
    def gain(k, shape):
        return 1.0 + 0.05 * jax.random.normal(k, shape, f32)
    return {
        "x": jax.random.normal(ks[0], (BATCH, SEQ, D_MODEL), f32),
        "p": jax.random.normal(ks[1], (DEPTH, BATCH, SEQ, PLE_DIM), f32),
        "mix_pre_norm": gain(ks[2], (L, D_MODEL)),
        "mix_post_norm": gain(ks[3], (L, D_MODEL)),
        "w_in": nrm(ks[4], (L, D_MODEL, D_IN), D_MODEL ** -0.5),
        "b_forget": jax.random.uniform(ks[5], (L, N_ATT_HEADS), f32, 2.0, 6.0),
        "gm_v_norm": gain(ks[6], (L, D_GM)),
        "gm_w_s": nrm(ks[7], (L, N_GM_GROUPS, CHUNK, CHUNK), 0.5 * CHUNK ** -0.5),
        "gm_b_s": 1.0 + 0.05 * jax.random.normal(ks[8], (L, N_GM_GROUPS, CHUNK), f32),
        "mix_out_norm": gain(ks[9], (L, D_MIX)),
        "w_out": nrm(ks[10], (L, D_MIX, D_MODEL), D_MIX ** -0.5),
        "ffn_pre_norm": gain(ks[11], (L, D_MODEL)),
        "ffn_post_norm": gain(ks[12], (L, D_MODEL)),
        "w_ffn_in": nrm(ks[13], (L, D_MODEL, 2 * D_FF), D_MODEL ** -0.5),
        "w_ffn_out": nrm(ks[14], (L, D_FF, D_MODEL), D_FF ** -0.5),
        "w_ple": nrm(ks[15], (L, PLE_DIM, D_MODEL), PLE_DIM ** -0.5),
        "ple_norm": gain(ks[16], (L, D_MODEL)),
        "w_ple_gate": nrm(ks[17], (L, D_MODEL, D_MODEL), D_MODEL ** -0.5),
    }


def reference(x, p, mix_pre_norm, mix_post_norm, w_in, b_forget, gm_v_norm, gm_w_s,
              gm_b_s, mix_out_norm, w_out, ffn_pre_norm, ffn_post_norm, w_ffn_in,
              w_ffn_out, w_ple, ple_norm, w_ple_gate):
    B, S, _ = x.shape
    h = x
    for i in range(DEPTH):
        hn = rmsnorm(h, mix_pre_norm[i])
        z = hn @ w_in[i]
        q, k, v, f_logit, g = jnp.split(
            z, [D_ATT, 2 * D_ATT, 3 * D_ATT, 3 * D_ATT + N_ATT_HEADS], axis=-1)
        q = q.reshape(B, S, N_ATT_HEADS, HEAD_DIM)
        k = k.reshape(B, S, N_ATT_HEADS, HEAD_DIM)
        v = v.reshape(B, S, N_ATT_HEADS, HEAD_DIM)
        log_f = jax.nn.log_sigmoid(f_logit.astype(jnp.float32) + b_forget[i].astype(jnp.float32))
        att = fox_attention(q, k, v, log_f).reshape(B, S, D_ATT)
        gm = chunked_spatial_gating(g, gm_w_s[i], gm_b_s[i], gm_v_norm[i])
        g_att, g_gm = jnp.split(mix_out_norm[i], [D_ATT])
        mixed = jnp.concatenate([rmsnorm(att, g_att), rmsnorm(gm, g_gm)], axis=-1)
        h = h + rmsnorm(mixed @ w_out[i], mix_post_norm[i])
        hn = rmsnorm(h, ffn_pre_norm[i])
        a, b = jnp.split(hn @ w_ffn_in[i], 2, axis=-1)
        h = h + rmsnorm((jax.nn.silu(a) * b) @ w_ffn_out[i], ffn_post_norm[i])
        e = rmsnorm(p[i] @ w_ple[i], ple_norm[i])
        gate = jax.nn.sigmoid(rmsnorm(h) @ w_ple_gate[i])
        h = h + gate * e
    return h
```

```python
import functools

import numpy as np
import jax
import jax.numpy as jnp
from jax import lax
from jax.experimental import pallas as pl
from jax.experimental.pallas import tpu as pltpu

D_MODEL = 1024
HEAD_DIM = 64
N_HEADS = 8
D_ATT = N_HEADS * HEAD_DIM
N_PAIRS = N_HEADS // 2
LANES = 128
N_GROUPS = 8
GROUP_DIM = 64
D_GM = N_GROUPS * GROUP_DIM
CHUNK = 128
D_FF = 2816
PLE_DIM = 256
EPS = 1e-6
NEG_INF = -1e30
N_SPLIT = 3
EXT = 2 * LANES

TM_IN = 256
TQ = 512
TK = 512
TM_POST = 256
VMEM_LIMIT = 56 * 1024 * 1024

F32 = jnp.float32
BF16 = jnp.bfloat16


def _rms(x):
    return x * lax.rsqrt(jnp.mean(x * x, axis=-1, keepdims=True) + EPS)


def _split3(x):
    hi = x.astype(BF16)
    r1 = x - hi.astype(F32)
    mid = r1.astype(BF16)
    lo = (r1 - mid.astype(F32)).astype(BF16)
    return jnp.concatenate([hi, mid, lo], axis=1)


def _dot(a, b):
    return jnp.dot(a, b, preferred_element_type=F32)


def _mix_in_kernel(h_ref, gain_ref, wqkv_ref, wf_ref, bf_ref, wg_ref, tril_ref, place_ref,
                   constq_ref, constk_ref, vgain_ref, ws_ref, bs_ref, ggm_ref,
                   q_ref, k_ref, v_ref, gm_ref, carry_ref, gm_scr):
    tm = h_ref.shape[1]

    @pl.when(pl.program_id(1) == 0)
    def _():
        carry_ref[...] = jnp.zeros_like(carry_ref)

    hb = (_rms(h_ref[0]) * gain_ref[...]).astype(BF16)

    zf = _dot(hb, wf_ref[...]) + bf_ref[...]
    logf = jnp.minimum(zf, 0.0) - jnp.log1p(jnp.exp(-jnp.abs(zf)))
    lane = lax.broadcasted_iota(jnp.int32, (1, LANES), 1)
    logf = jnp.where(lane < N_HEADS, logf, 0.0)
    cs = _dot(tril_ref[...], _split3(logf))
    c = cs[:, :LANES] + cs[:, LANES:2 * LANES] + cs[:, 2 * LANES:] + carry_ref[...]
    carry_ref[...] = c[tm - 1:tm, :]
    ext = _dot(_split3(c), place_ref[...])
    xq = (ext[:, :N_PAIRS * LANES] + constq_ref[...]).astype(BF16)
    xk = (ext[:, N_PAIRS * LANES:] + constk_ref[...]).astype(BF16)

    zqkv = _dot(hb, wqkv_ref[...])
    zq = (zqkv[:, :D_ATT] * (HEAD_DIM ** -0.5)).astype(BF16)
    zk = zqkv[:, D_ATT:2 * D_ATT].astype(BF16)
    v_ref[0] = zqkv[:, 2 * D_ATT:].astype(BF16)
    for j in range(N_PAIRS):
        pair = slice(j * LANES, (j + 1) * LANES)
        q_ref[0, :, j * EXT:j * EXT + LANES] = zq[:, pair]
        q_ref[0, :, j * EXT + LANES:(j + 1) * EXT] = xq[:, pair]
        k_ref[0, :, j * EXT:j * EXT + LANES] = zk[:, pair]
        k_ref[0, :, j * EXT + LANES:(j + 1) * EXT] = xk[:, pair]

    ge = jax.nn.gelu(_dot(hb, wg_ref[...]))
    first = lane < GROUP_DIM
    row = lax.broadcasted_iota(jnp.int32, (2 * CHUNK, CHUNK), 0)
    col = lax.broadcasted_iota(jnp.int32, (2 * CHUNK, CHUNK), 1)
    causal = (row % CHUNK) >= col
    for j in range(N_PAIRS):
        pair = slice(j * LANES, (j + 1) * LANES)
        u = ge[:, pair]
        x = ge[:, D_GM + j * LANES:D_GM + (j + 1) * LANES]
        s_a = jnp.sum(jnp.where(first, x, 0.0), axis=-1, keepdims=True)
        s_b = jnp.sum(jnp.where(first, 0.0, x), axis=-1, keepdims=True)
        d = x - jnp.where(first, s_a, s_b) * (1.0 / GROUP_DIM)
        d2 = d * d
        v_a = jnp.sum(jnp.where(first, d2, 0.0), axis=-1, keepdims=True)
        v_b = jnp.sum(jnp.where(first, 0.0, d2), axis=-1, keepdims=True)
        var = jnp.where(first, v_a, v_b) * (1.0 / GROUP_DIM)
        vn = (d * lax.rsqrt(var + EPS) * vgain_ref[:, pair]).astype(BF16)
        w2 = jnp.where(causal, ws_ref[j], 0.0).astype(BF16)
        for n in range(tm // CHUNK):
            rows = slice(n * CHUNK, (n + 1) * CHUNK)
            mm = _dot(w2, vn[rows])
            mixed = jnp.where(first, mm[:CHUNK], mm[CHUNK:]) + bs_ref[:, pair]
            gm_scr[rows, pair] = u[rows] * mixed
    gm_ref[0] = (_rms(gm_scr[...]) * ggm_ref[...]).astype(BF16)


def _mix_in(h, gain, wqkv, wf, bf, wg, tril, place, constq, constk, vgain, ws, bs, ggm):
    B, S, _ = h.shape
    tm = TM_IN
    const = lambda shape: pl.BlockSpec(shape, lambda b, s: (0,) * len(shape))
    tile = lambda w: pl.BlockSpec((1, tm, w), lambda b, s: (b, s, 0))
    return pl.pallas_call(
        _mix_in_kernel,
        grid=(B, S // tm),
        in_specs=[
            tile(D_MODEL), const((1, D_MODEL)), const((D_MODEL, 3 * D_ATT)),
            const((D_MODEL, LANES)), const((1, LANES)), const((D_MODEL, 2 * D_GM)),
            const((tm, tm)), const((N_SPLIT * LANES, 2 * N_PAIRS * LANES)),
            const((1, N_PAIRS * LANES)), const((1, N_PAIRS * LANES)), const((1, D_GM)),
            const((N_PAIRS, 2 * CHUNK, CHUNK)), const((CHUNK, D_GM)), const((1, D_GM)),
        ],
        out_specs=[tile(N_PAIRS * EXT), tile(N_PAIRS * EXT), tile(D_ATT), tile(D_GM)],
        out_shape=[
            jax.ShapeDtypeStruct((B, S, N_PAIRS * EXT), BF16),
            jax.ShapeDtypeStruct((B, S, N_PAIRS * EXT), BF16),
            jax.ShapeDtypeStruct((B, S, D_ATT), BF16),
            jax.ShapeDtypeStruct((B, S, D_GM), BF16),
        ],
        scratch_shapes=[pltpu.VMEM((1, LANES), F32), pltpu.VMEM((tm, D_GM), F32)],
        compiler_params=pltpu.CompilerParams(
            dimension_semantics=("arbitrary", "arbitrary"), vmem_limit_bytes=VMEM_LIMIT),
        name="mix_in",
    )(h, gain, wqkv, wf, bf, wg, tril, place, constq, constk, vgain, ws, bs, ggm)


def _attn_kernel(q_ref, k_ref, v_ref, o_ref, qh_ref, m_ref, l_ref, acc_ref):
    tq = q_ref.shape[1]
    i = pl.program_id(2)
    q = q_ref[0]
    lane = lax.broadcasted_iota(jnp.int32, (1, EXT), 1)
    n_bias = 2 * N_SPLIT
    own = ((lane < HEAD_DIM) | ((lane >= LANES) & (lane < LANES + n_bias)),
           ((lane >= HEAD_DIM) & (lane < LANES)) | ((lane >= LANES + n_bias) & (lane < LANES + 2 * n_bias)))
    for hh in range(2):
        qh_ref[hh] = jnp.where(own[hh], q, jnp.zeros_like(q))
    m_ref[...] = jnp.full(m_ref.shape, NEG_INF, F32)
    l_ref[...] = jnp.zeros(l_ref.shape, F32)
    acc_ref[...] = jnp.zeros(acc_ref.shape, F32)

    def step(j, diagonal):
        start = pl.multiple_of(j * TK, TK)
        k = k_ref[0, pl.ds(start, TK), :]
        v = v_ref[0, pl.ds(start, TK), :]
        for hh in range(2):
            s = lax.dot_general(qh_ref[hh], k, (((1,), (1,)), ((), ())),
                                preferred_element_type=F32)
            if diagonal:
                row = lax.broadcasted_iota(jnp.int32, (tq, TK), 0)
                col = lax.broadcasted_iota(jnp.int32, (tq, TK), 1)
                s = jnp.where(row >= col, s, NEG_INF)
            m_prev = m_ref[hh]
            m_next = jnp.maximum(m_prev, jnp.max(s, axis=1, keepdims=True))
            p = jnp.exp(s - pltpu.repeat(m_next, TK // LANES, axis=1))
            alpha = jnp.exp(m_prev - m_next)
            l_ref[hh] = alpha * l_ref[hh] + jnp.sum(p, axis=1, keepdims=True)
            acc_ref[hh] = alpha * acc_ref[hh] + _dot(p.astype(BF16), v)
            m_ref[hh] = m_next

    def body(j, carry):
        step(j, False)
        return carry

    lax.fori_loop(0, i, body, 0)
    step(i, True)
    first = lax.broadcasted_iota(jnp.int32, (1, LANES), 1) < HEAD_DIM
    o_ref[0] = jnp.where(first, acc_ref[0] / l_ref[0], acc_ref[1] / l_ref[1])


def _attn(q_ext, k_ext, v):
    B, S, _ = v.shape
    assert TQ == TK
    return pl.pallas_call(
        _attn_kernel,
        grid=(B, N_PAIRS, S // TQ),
        in_specs=[
            pl.BlockSpec((1, TQ, EXT), lambda b, j, i: (b, i, j)),
            pl.BlockSpec((1, S, EXT), lambda b, j, i: (b, 0, j)),
            pl.BlockSpec((1, S, LANES), lambda b, j, i: (b, 0, j)),
        ],
        out_specs=pl.BlockSpec((1, TQ, LANES), lambda b, j, i: (b, i, j)),
        out_shape=jax.ShapeDtypeStruct((B, S, D_ATT), F32),
        scratch_shapes=[
            pltpu.VMEM((2, TQ, EXT), BF16),
            pltpu.VMEM((2, TQ, LANES), F32),
            pltpu.VMEM((2, TQ, LANES), F32),
            pltpu.VMEM((2, TQ, LANES), F32),
        ],
        compiler_params=pltpu.CompilerParams(
            dimension_semantics=("arbitrary", "arbitrary", "arbitrary"),
            vmem_limit_bytes=VMEM_LIMIT),
        name="attn",
    )(q_ext, k_ext, v)


def _post_kernel(h_ref, att_ref, gm_ref, p_ref, gatt_ref, wout_ref, gpost_ref, gpre_ref,
                 wa_ref, wb_ref, wfo_ref, gfpost_ref, wple_ref, gple_ref, wgate_ref, o_ref):
    an = (_rms(att_ref[...]) * gatt_ref[...]).astype(BF16)
    mixed = jnp.concatenate([an, gm_ref[...]], axis=1)
    h1 = h_ref[...] + _rms(_dot(mixed, wout_ref[...])) * gpost_ref[...]
    hn = (_rms(h1) * gpre_ref[...]).astype(BF16)
    a = _dot(hn, wa_ref[...])
    b = _dot(hn, wb_ref[...])
    act = (jax.nn.silu(a) * b).astype(BF16)
    h2 = h1 + _rms(_dot(act, wfo_ref[...])) * gfpost_ref[...]
    e = _rms(_dot(p_ref[...].astype(BF16), wple_ref[...])) * gple_ref[...]
    gate = jax.nn.sigmoid(_dot(_rms(h2).astype(BF16), wgate_ref[...]))
    o_ref[...] = h2 + gate * e


def _post(h, att, gm, p, gatt, wout, gpost, gpre, wa, wb, wfo, gfpost, wple, gple, wgate):
    N = h.shape[0]
    tm = TM_POST
    const = lambda shape: pl.BlockSpec(shape, lambda r: (0, 0), pipeline_mode=pl.Buffered(1))
    tile = lambda w: pl.BlockSpec((tm, w), lambda r: (r, 0))
    return pl.pallas_call(
        _post_kernel,
        grid=(N // tm,),
        in_specs=[
            tile(D_MODEL), tile(D_ATT), tile(D_GM), tile(PLE_DIM),
            const((1, D_ATT)), const((D_MODEL, D_MODEL)), const((1, D_MODEL)), const((1, D_MODEL)),
            const((D_MODEL, D_FF)), const((D_MODEL, D_FF)), const((D_FF, D_MODEL)),
            const((1, D_MODEL)), const((PLE_DIM, D_MODEL)), const((1, D_MODEL)),
            const((D_MODEL, D_MODEL)),
        ],
        out_specs=tile(D_MODEL),
        out_shape=jax.ShapeDtypeStruct((N, D_MODEL), F32),
        compiler_params=pltpu.CompilerParams(
            dimension_semantics=("arbitrary",), vmem_limit_bytes=VMEM_LIMIT),
        name="post",
    )(h, att, gm, p, gatt, wout, gpost, gpre, wa, wb, wfo, gfpost, wple, gple, wgate)


def _placement():
    nq = N_PAIRS * LANES
    place = np.zeros((N_SPLIT * LANES, 2 * nq), np.float32)
    constq = np.zeros((1, nq), np.float32)
    constk = np.zeros((1, nq), np.float32)
    for h in range(N_HEADS):
        base = LANES * (h // 2) + 2 * N_SPLIT * (h % 2)
        for t in range(N_SPLIT):
            place[t * LANES + h, base + t] = 1.0
            place[t * LANES + h, nq + base + N_SPLIT + t] = -1.0
            constq[0, base + N_SPLIT + t] = 1.0
            constk[0, base + t] = 1.0
    return place, constq, constk


def kernel(x, p, mix_pre_norm, mix_post_norm, w_in, b_forget, gm_v_norm, gm_w_s, gm_b_s,
           mix_out_norm, w_out, ffn_pre_norm, ffn_post_norm, w_ffn_in, w_ffn_out, w_ple,
           ple_norm, w_ple_gate):
    B, S, D = x.shape
    depth = w_in.shape[0]
    place, constq, constk = _placement()
    place = jnp.asarray(place, BF16)
    constq = jnp.asarray(constq)
    constk = jnp.asarray(constk)
    tril = jnp.asarray(np.tril(np.ones((TM_IN, TM_IN), np.float32)), BF16)
    row = lambda g: g.reshape(1, -1)

    h = x
    for i in range(depth):
        wi = w_in[i]
        wqkv = wi[:, :3 * D_ATT].astype(BF16)
        wf = jnp.pad(wi[:, 3 * D_ATT:3 * D_ATT + N_HEADS], ((0, 0), (0, LANES - N_HEADS))).astype(BF16)
        wg = wi[:, 3 * D_ATT + N_HEADS:].astype(BF16)
        bf = jnp.pad(b_forget[i], (0, LANES - N_HEADS)).reshape(1, LANES)
        ws = gm_w_s[i].reshape(N_PAIRS, 2 * CHUNK, CHUNK)
        bs = jnp.repeat(gm_b_s[i].T, GROUP_DIM, axis=1)
        q_ext, k_ext, v, gm = _mix_in(
            h, row(mix_pre_norm[i]), wqkv, wf, bf, wg, tril, place, constq, constk,
            row(gm_v_norm[i]), ws, bs, row(mix_out_norm[i, D_ATT:]))
        att = _attn(q_ext, k_ext, v)
        h = _post(
            h.reshape(B * S, D), att.reshape(B * S, D_ATT), gm.reshape(B * S, D_GM),
            p[i].reshape(B * S, PLE_DIM), row(mix_out_norm[i, :D_ATT]), w_out[i].astype(BF16),
            row(mix_post_norm[i]), row(ffn_pre_norm[i]), w_ffn_in[i, :, :D_FF].astype(BF16),
            w_ffn_in[i, :, D_FF:].astype(BF16), w_ffn_out[i].astype(BF16), row(ffn_post_norm[i]),
            w_ple[i].astype(BF16), row(ple_norm[i]), w_ple_gate[i].astype(BF16),
        ).reshape(B, S, D)
    return h
```

```python
import numpy as np
import jax
import jax.numpy as jnp
from jax import lax
from jax.experimental import pallas as pl
from jax.experimental.pallas import tpu as pltpu

D_MODEL = 1024
HEAD_DIM = 64
N_HEADS = 8
D_ATT = N_HEADS * HEAD_DIM
N_PAIRS = N_HEADS // 2
LANES = 128
N_GROUPS = 8
GROUP_DIM = 64
D_GM = N_GROUPS * GROUP_DIM
CHUNK = 128
D_FF = 2816
PLE_DIM = 256
EPS = 1e-6
NEG_INF = -1e30
N_SPLIT = 3
N_BIAS = 2 * N_SPLIT
EXT = 2 * LANES

TM_IN = 256
TQ = 512
KC = 256
ONES_ROWS = 16
ACC_ROWS = HEAD_DIM + ONES_ROWS
TM_POST = 256
VMEM_LIMIT = 56 * 1024 * 1024

F32 = jnp.float32
BF16 = jnp.bfloat16
NT = (((1,), (1,)), ((), ()))


def _rms(x):
    return x * lax.rsqrt(jnp.mean(x * x, axis=-1, keepdims=True) + EPS)


def _split3(x):
    hi = x.astype(BF16)
    r1 = x - hi.astype(F32)
    mid = r1.astype(BF16)
    lo = (r1 - mid.astype(F32)).astype(BF16)
    return jnp.concatenate([hi, mid, lo], axis=1)


def _dot(a, b):
    return jnp.dot(a, b, preferred_element_type=F32)


def _dot_nt(a, b):
    return lax.dot_general(a, b, NT, preferred_element_type=F32)


def _mix_in_kernel(h_ref, gain_ref, wqv_ref, wk_ref, wf_ref, bf_ref, wg_ref, tril_ref,
                   placeq_ref, placek_ref, constq_ref, constk_ref, vgain_ref, ws_ref, bs_ref,
                   ggm_ref, qt_ref, xqt_ref, k_ref, xk_ref, vt_ref, gm_ref, carry_ref, gm_scr):
    tm = h_ref.shape[1]

    @pl.when(pl.program_id(1) == 0)
    def _():
        carry_ref[...] = jnp.zeros_like(carry_ref)

    hb = (_rms(h_ref[0]) * gain_ref[...]).astype(BF16)

    zf = _dot(hb, wf_ref[...]) + bf_ref[...]
    logf = jnp.minimum(zf, 0.0) - jnp.log1p(jnp.exp(-jnp.abs(zf)))
    lane = lax.broadcasted_iota(jnp.int32, (1, LANES), 1)
    logf = jnp.where(lane < N_HEADS, logf, 0.0)
    cs = _dot(tril_ref[...], _split3(logf))
    c = cs[:, :LANES] + cs[:, LANES:2 * LANES] + cs[:, 2 * LANES:] + carry_ref[...]
    carry_ref[...] = c[tm - 1:tm, :]
    c3 = _split3(c)
    xqt_ref[0] = (_dot_nt(placeq_ref[...], c3) + constq_ref[...]).astype(BF16)
    xk_ref[0] = (_dot(c3, placek_ref[...]) + constk_ref[...]).astype(BF16)

    zqv = _dot_nt(wqv_ref[...], hb)
    qt_ref[0] = (zqv[:D_ATT] * (HEAD_DIM ** -0.5)).astype(BF16)
    vt = zqv[D_ATT:].astype(BF16)
    for n in range(tm // KC):
        vt_ref[0, n] = vt[:, n * KC:(n + 1) * KC]
    k_ref[0] = _dot(hb, wk_ref[...]).astype(BF16)

    ge = jax.nn.gelu(_dot(hb, wg_ref[...]))
    first = lane < GROUP_DIM
    row = lax.broadcasted_iota(jnp.int32, (2 * CHUNK, CHUNK), 0)
    col = lax.broadcasted_iota(jnp.int32, (2 * CHUNK, CHUNK), 1)
    causal = (row % CHUNK) >= col
    for j in range(N_PAIRS):
        pair = slice(j * LANES, (j + 1) * LANES)
        u = ge[:, pair]
        x = ge[:, D_GM + j * LANES:D_GM + (j + 1) * LANES]
        s_a = jnp.sum(jnp.where(first, x, 0.0), axis=-1, keepdims=True)
        s_b = jnp.sum(jnp.where(first, 0.0, x), axis=-1, keepdims=True)
        d = x - jnp.where(first, s_a, s_b) * (1.0 / GROUP_DIM)
        d2 = d * d
        v_a = jnp.sum(jnp.where(first, d2, 0.0), axis=-1, keepdims=True)
        v_b = jnp.sum(jnp.where(first, 0.0, d2), axis=-1, keepdims=True)
        var = jnp.where(first, v_a, v_b) * (1.0 / GROUP_DIM)
        vn = (d * lax.rsqrt(var + EPS) * vgain_ref[:, pair]).astype(BF16)
        w2 = jnp.where(causal, ws_ref[j], 0.0).astype(BF16)
        for n in range(tm // CHUNK):
            rows = slice(n * CHUNK, (n + 1) * CHUNK)
            mm = _dot(w2, vn[rows])
            mixed = jnp.where(first, mm[:CHUNK], mm[CHUNK:]) + bs_ref[:, pair]
            gm_scr[rows, pair] = u[rows] * mixed
    gm_ref[0] = (_rms(gm_scr[...]) * ggm_ref[...]).astype(BF16)


def _mix_in(h, gain, wqv, wk, wf, bf, wg, tril, placeq, placek, constq, constk, vgain, ws, bs, ggm):
    B, S, _ = h.shape
    tm = TM_IN
    const = lambda shape: pl.BlockSpec(shape, lambda b, s: (0,) * len(shape))
    rows = lambda w: pl.BlockSpec((1, tm, w), lambda b, s: (b, s, 0))
    cols = lambda w: pl.BlockSpec((1, w, tm), lambda b, s: (b, 0, s))
    return pl.pallas_call(
        _mix_in_kernel,
        grid=(B, S // tm),
        in_specs=[
            rows(D_MODEL), const((1, D_MODEL)), const((2 * D_ATT, D_MODEL)),
            const((D_MODEL, D_ATT)), const((D_MODEL, LANES)), const((1, LANES)),
            const((D_MODEL, 2 * D_GM)), const((tm, tm)),
            const((LANES, N_SPLIT * LANES)), const((N_SPLIT * LANES, LANES)),
            const((LANES, 1)), const((1, LANES)), const((1, D_GM)),
            const((N_PAIRS, 2 * CHUNK, CHUNK)), const((CHUNK, D_GM)), const((1, D_GM)),
        ],
        out_specs=[
            cols(D_ATT), cols(LANES), rows(D_ATT), rows(LANES),
            pl.BlockSpec((1, tm // KC, D_ATT, KC), lambda b, s: (b, s, 0, 0)),
            rows(D_GM),
        ],
        out_shape=[
            jax.ShapeDtypeStruct((B, D_ATT, S), BF16),
            jax.ShapeDtypeStruct((B, LANES, S), BF16),
            jax.ShapeDtypeStruct((B, S, D_ATT), BF16),
            jax.ShapeDtypeStruct((B, S, LANES), BF16),
            jax.ShapeDtypeStruct((B, S // KC, D_ATT, KC), BF16),
            jax.ShapeDtypeStruct((B, S, D_GM), BF16),
        ],
        scratch_shapes=[pltpu.VMEM((1, LANES), F32), pltpu.VMEM((tm, D_GM), F32)],
        compiler_params=pltpu.CompilerParams(
            dimension_semantics=("arbitrary", "arbitrary"), vmem_limit_bytes=VMEM_LIMIT),
        name="mix_in",
    )(h, gain, wqv, wk, wf, bf, wg, tril, placeq, placek, constq, constk, vgain, ws, bs, ggm)


def _attn_kernel(qt_ref, xqt_ref, k_ref, xk_ref, vt_ref, o_ref, qh_ref, acc_ref, st_ref):
    tq = qt_ref.shape[2]
    pair = pl.program_id(1)
    i = pl.program_id(2)
    qt = jnp.concatenate([qt_ref[0], xqt_ref[0]], axis=0)
    row = lax.broadcasted_iota(jnp.int32, (EXT, 1), 0)
    for hh in range(2):
        slot = LANES + N_BIAS * (2 * pair + hh)
        own = ((row >= hh * HEAD_DIM) & (row < (hh + 1) * HEAD_DIM)) | ((row >= slot) & (row < slot + N_BIAS))
        qh_ref[hh] = jnp.where(own, qt, jnp.zeros_like(qt))
    acc_ref[...] = jnp.zeros(acc_ref.shape, F32)
    ones = (lax.broadcasted_iota(jnp.int32, (ONES_ROWS, KC), 0) == 0).astype(BF16)

    def scores(c, slot):
        start = pl.multiple_of(c * KC, KC)
        k = jnp.concatenate([k_ref[0, pl.ds(start, KC), :], xk_ref[0, pl.ds(start, KC), :]], axis=1)
        for hh in range(2):
            st_ref[slot, hh] = _dot(k, qh_ref[hh])

    def softmax_pv(c, slot, m, diag):
        vt = vt_ref[0, c]
        out = []
        for hh in range(2):
            st = st_ref[slot, hh]
            if diag is not None:
                key = lax.broadcasted_iota(jnp.int32, (KC, tq), 0) + diag * KC
                qry = lax.broadcasted_iota(jnp.int32, (KC, tq), 1)
                st = jnp.where(key <= qry, st, NEG_INF)
            m_next = jnp.maximum(m[hh], jnp.max(st, axis=0, keepdims=True))
            p = jnp.exp(st - m_next).astype(BF16)
            alpha = jnp.exp(m[hh] - m_next)
            vx = jnp.concatenate([vt[hh * HEAD_DIM:(hh + 1) * HEAD_DIM], ones], axis=0)
            acc_ref[hh] = alpha * acc_ref[hh] + _dot(vx, p)
            out.append(m_next)
        return tuple(out)

    def body(t, m):
        scores(2 * t + 1, 1)
        m = softmax_pv(2 * t, 0, m, None)
        scores(2 * t + 2, 0)
        return softmax_pv(2 * t + 1, 1, m, None)

    assert tq == 2 * KC
    init = jnp.full((1, tq), NEG_INF, F32)
    scores(0, 0)
    m = lax.fori_loop(0, i, body, (init, init))
    scores(2 * i + 1, 1)
    m = softmax_pv(2 * i, 0, m, 0)
    m = softmax_pv(2 * i + 1, 1, m, 1)
    out_t = jnp.concatenate(
        [acc_ref[hh, :HEAD_DIM] / acc_ref[hh, HEAD_DIM:HEAD_DIM + 1] for hh in range(2)], axis=0)
    o_ref[0] = out_t.T


def _attn(qt, xqt, k, xk, vt):
    B, S, _ = k.shape
    return pl.pallas_call(
        _attn_kernel,
        grid=(B, N_PAIRS, S // TQ),
        in_specs=[
            pl.BlockSpec((1, LANES, TQ), lambda b, j, i: (b, j, i)),
            pl.BlockSpec((1, LANES, TQ), lambda b, j, i: (b, 0, i)),
            pl.BlockSpec((1, S, LANES), lambda b, j, i: (b, 0, j)),
            pl.BlockSpec((1, S, LANES), lambda b, j, i: (b, 0, 0)),
            pl.BlockSpec((1, S // KC, LANES, KC), lambda b, j, i: (b, 0, j, 0)),
        ],
        out_specs=pl.BlockSpec((1, TQ, LANES), lambda b, j, i: (b, i, j)),
        out_shape=jax.ShapeDtypeStruct((B, S, D_ATT), F32),
        scratch_shapes=[
            pltpu.VMEM((2, EXT, TQ), BF16),
            pltpu.VMEM((2, ACC_ROWS, TQ), F32),
            pltpu.VMEM((2, 2, KC, TQ), F32),
        ],
        compiler_params=pltpu.CompilerParams(
            dimension_semantics=("arbitrary", "arbitrary", "arbitrary"),
            vmem_limit_bytes=VMEM_LIMIT),
        name="attn",
    )(qt, xqt, k, xk, vt)


def _post_kernel(h_ref, att_ref, gm_ref, p_ref, gatt_ref, wout_ref, gpost_ref, gpre_ref,
                 wa_ref, wb_ref, wfo_ref, gfpost_ref, wple_ref, gple_ref, wgate_ref, o_ref):
    an = (_rms(att_ref[...]) * gatt_ref[...]).astype(BF16)
    mixed = jnp.concatenate([an, gm_ref[...]], axis=1)
    h1 = h_ref[...] + _rms(_dot(mixed, wout_ref[...])) * gpost_ref[...]
    hn = (_rms(h1) * gpre_ref[...]).astype(BF16)
    a = _dot(hn, wa_ref[...])
    b = _dot(hn, wb_ref[...])
    act = (jax.nn.silu(a) * b).astype(BF16)
    h2 = h1 + _rms(_dot(act, wfo_ref[...])) * gfpost_ref[...]
    e = _rms(_dot(p_ref[...].astype(BF16), wple_ref[...])) * gple_ref[...]
    gate = jax.nn.sigmoid(_dot(_rms(h2).astype(BF16), wgate_ref[...]))
    o_ref[...] = h2 + gate * e


def _post(h, att, gm, p, gatt, wout, gpost, gpre, wa, wb, wfo, gfpost, wple, gple, wgate):
    N = h.shape[0]
    tm = TM_POST
    const = lambda shape: pl.BlockSpec(shape, lambda r: (0, 0), pipeline_mode=pl.Buffered(1))
    tile = lambda w: pl.BlockSpec((tm, w), lambda r: (r, 0))
    return pl.pallas_call(
        _post_kernel,
        grid=(N // tm,),
        in_specs=[
            tile(D_MODEL), tile(D_ATT), tile(D_GM), tile(PLE_DIM),
            const((1, D_ATT)), const((D_MODEL, D_MODEL)), const((1, D_MODEL)), const((1, D_MODEL)),
            const((D_MODEL, D_FF)), const((D_MODEL, D_FF)), const((D_FF, D_MODEL)),
            const((1, D_MODEL)), const((PLE_DIM, D_MODEL)), const((1, D_MODEL)),
            const((D_MODEL, D_MODEL)),
        ],
        out_specs=tile(D_MODEL),
        out_shape=jax.ShapeDtypeStruct((N, D_MODEL), F32),
        compiler_params=pltpu.CompilerParams(
            dimension_semantics=("arbitrary",), vmem_limit_bytes=VMEM_LIMIT),
        name="post",
    )(h, att, gm, p, gatt, wout, gpost, gpre, wa, wb, wfo, gfpost, wple, gple, wgate)


def _placement():
    placeq = np.zeros((LANES, N_SPLIT * LANES), np.float32)
    placek = np.zeros((N_SPLIT * LANES, LANES), np.float32)
    constq = np.zeros((LANES, 1), np.float32)
    constk = np.zeros((1, LANES), np.float32)
    for h in range(N_HEADS):
        for t in range(N_SPLIT):
            placeq[N_BIAS * h + t, t * LANES + h] = 1.0
            placek[t * LANES + h, N_BIAS * h + N_SPLIT + t] = -1.0
            constq[N_BIAS * h + N_SPLIT + t, 0] = 1.0
            constk[0, N_BIAS * h + t] = 1.0
    return placeq, placek, constq, constk


def kernel(x, p, mix_pre_norm, mix_post_norm, w_in, b_forget, gm_v_norm, gm_w_s, gm_b_s,
           mix_out_norm, w_out, ffn_pre_norm, ffn_post_norm, w_ffn_in, w_ffn_out, w_ple,
           ple_norm, w_ple_gate):
    B, S, D = x.shape
    depth = w_in.shape[0]
    placeq, placek, constq, constk = _placement()
    placeq = jnp.asarray(placeq, BF16)
    placek = jnp.asarray(placek, BF16)
    constq = jnp.asarray(constq)
    constk = jnp.asarray(constk)
    tril = jnp.asarray(np.tril(np.ones((TM_IN, TM_IN), np.float32)), BF16)
    row = lambda g: g.reshape(1, -1)

    h = x
    for i in range(depth):
        wi = w_in[i]
        wqv = jnp.concatenate([wi[:, :D_ATT], wi[:, 2 * D_ATT:3 * D_ATT]], axis=1).T.astype(BF16)
        wk = wi[:, D_ATT:2 * D_ATT].astype(BF16)
        wf = jnp.pad(wi[:, 3 * D_ATT:3 * D_ATT + N_HEADS], ((0, 0), (0, LANES - N_HEADS))).astype(BF16)
        wg = wi[:, 3 * D_ATT + N_HEADS:].astype(BF16)
        bf = jnp.pad(b_forget[i], (0, LANES - N_HEADS)).reshape(1, LANES)
        ws = gm_w_s[i].reshape(N_PAIRS, 2 * CHUNK, CHUNK)
        bs = jnp.repeat(gm_b_s[i].T, GROUP_DIM, axis=1)
        qt, xqt, k, xk, vt, gm = _mix_in(
            h, row(mix_pre_norm[i]), wqv, wk, wf, bf, wg, tril, placeq, placek, constq, constk,
            row(gm_v_norm[i]), ws, bs, row(mix_out_norm[i, D_ATT:]))
        att = _attn(qt, xqt, k, xk, vt)
        h = _post(
            h.reshape(B * S, D), att.reshape(B * S, D_ATT), gm.reshape(B * S, D_GM),
            p[i].reshape(B * S, PLE_DIM), row(mix_out_norm[i, :D_ATT]), w_out[i].astype(BF16),
            row(mix_post_norm[i]), row(ffn_pre_norm[i]), w_ffn_in[i, :, :D_FF].astype(BF16),
            w_ffn_in[i, :, D_FF:].astype(BF16), w_ffn_out[i].astype(BF16), row(ffn_post_norm[i]),
            w_ple[i].astype(BF16), row(ple_norm[i]), w_ple_gate[i].astype(BF16),
        ).reshape(B, S, D)
    return h
```

```python
import numpy as np
import jax
import jax.numpy as jnp
from jax import lax
from jax.experimental import pallas as pl
from jax.experimental.pallas import tpu as pltpu

D_MODEL = 1024
HEAD_DIM = 64
N_HEADS = 8
D_ATT = N_HEADS * HEAD_DIM
N_PAIRS = N_HEADS // 2
LANES = 128
N_GROUPS = 8
GROUP_DIM = 64
D_GM = N_GROUPS * GROUP_DIM
CHUNK = 128
D_FF = 2816
PLE_DIM = 256
EPS = 1e-6
NEG_INF = -1e30
LOG2E = 1.4426950408889634
N_SPLIT = 3
N_BIAS = 2 * N_SPLIT
EXT = 2 * LANES

TM_IN = 512
TQ = 1024
KC = 256
N_SLOT = TQ // KC
ONES_ROWS = 16
ACC_ROWS = HEAD_DIM + ONES_ROWS
TM_POST = 256
VMEM_LIMIT = 56 * 1024 * 1024

F32 = jnp.float32
BF16 = jnp.bfloat16
NT = (((1,), (1,)), ((), ()))


def _rms(x):
    return x * lax.rsqrt(jnp.mean(x * x, axis=-1, keepdims=True) + EPS)


def _split3(x):
    hi = x.astype(BF16)
    r1 = x - hi.astype(F32)
    mid = r1.astype(BF16)
    lo = (r1 - mid.astype(F32)).astype(BF16)
    return jnp.concatenate([hi, mid, lo], axis=1)


def _dot(a, b):
    return jnp.dot(a, b, preferred_element_type=F32)


def _dot_nt(a, b):
    return lax.dot_general(a, b, NT, preferred_element_type=F32)


def _mix_in_kernel(h_ref, gain_ref, wqv_ref, wk_ref, wf_ref, bf_ref, wg_ref, tril_ref,
                   placeq_ref, placek_ref, constq_ref, constk_ref, vgain_ref, ws_ref, bs_ref,
                   ggm_ref, qt_ref, xqt_ref, k_ref, xk_ref, vt_ref, gm_ref, carry_ref, gm_scr):
    tm = h_ref.shape[1]

    @pl.when(pl.program_id(1) == 0)
    def _():
        carry_ref[...] = jnp.zeros_like(carry_ref)

    hb = (_rms(h_ref[0]) * gain_ref[...]).astype(BF16)
    ge = jax.nn.gelu(_dot(hb, wg_ref[...]))

    zf = _dot(hb, wf_ref[...]) + bf_ref[...]
    logf = jnp.minimum(zf, 0.0) - jnp.log1p(jnp.exp(-jnp.abs(zf)))
    lane = lax.broadcasted_iota(jnp.int32, (1, LANES), 1)
    logf = jnp.where(lane < N_HEADS, logf, 0.0)
    cs = _dot(tril_ref[...], _split3(logf))
    c = cs[:, :LANES] + cs[:, LANES:2 * LANES] + cs[:, 2 * LANES:] + carry_ref[...]
    carry_ref[...] = c[tm - 1:tm, :]
    c3 = _split3(c * LOG2E)
    xqt_ref[0] = (_dot_nt(placeq_ref[...], c3) + constq_ref[...]).astype(BF16)
    xk_ref[0] = (_dot(c3, placek_ref[...]) + constk_ref[...]).astype(BF16)

    zqv = _dot_nt(wqv_ref[...], hb)
    qt_ref[0] = (zqv[:D_ATT] * (HEAD_DIM ** -0.5 * LOG2E)).astype(BF16)
    vt = zqv[D_ATT:].astype(BF16)
    for n in range(tm // KC):
        vt_ref[0, n] = vt[:, n * KC:(n + 1) * KC]
    k_ref[0] = _dot(hb, wk_ref[...]).astype(BF16)

    first = lane < GROUP_DIM
    row = lax.broadcasted_iota(jnp.int32, (2 * CHUNK, CHUNK), 0)
    col = lax.broadcasted_iota(jnp.int32, (2 * CHUNK, CHUNK), 1)
    causal = (row % CHUNK) >= col
    for j in range(N_PAIRS):
        pair = slice(j * LANES, (j + 1) * LANES)
        u = ge[:, pair]
        x = ge[:, D_GM + j * LANES:D_GM + (j + 1) * LANES]
        s_a = jnp.sum(jnp.where(first, x, 0.0), axis=-1, keepdims=True)
        s_b = jnp.sum(jnp.where(first, 0.0, x), axis=-1, keepdims=True)
        d = x - jnp.where(first, s_a, s_b) * (1.0 / GROUP_DIM)
        d2 = d * d
        v_a = jnp.sum(jnp.where(first, d2, 0.0), axis=-1, keepdims=True)
        v_b = jnp.sum(jnp.where(first, 0.0, d2), axis=-1, keepdims=True)
        var = jnp.where(first, v_a, v_b) * (1.0 / GROUP_DIM)
        vn = (d * lax.rsqrt(var + EPS) * vgain_ref[:, pair]).astype(BF16)
        w2 = jnp.where(causal, ws_ref[j], 0.0).astype(BF16)
        for n in range(tm // CHUNK):
            rows = slice(n * CHUNK, (n + 1) * CHUNK)
            mm = _dot(w2, vn[rows])
            mixed = jnp.where(first, mm[:CHUNK], mm[CHUNK:]) + bs_ref[:, pair]
            gm_scr[rows, pair] = u[rows] * mixed
    gm_ref[0] = (_rms(gm_scr[...]) * ggm_ref[...]).astype(BF16)


def _mix_in(h, gain, wqv, wk, wf, bf, wg, tril, placeq, placek, constq, constk, vgain, ws, bs, ggm):
    B, S, _ = h.shape
    tm = TM_IN
    const = lambda shape: pl.BlockSpec(shape, lambda b, s: (0,) * len(shape))
    rows = lambda w: pl.BlockSpec((1, tm, w), lambda b, s: (b, s, 0))
    cols = lambda w: pl.BlockSpec((1, w, tm), lambda b, s: (b, 0, s))
    return pl.pallas_call(
        _mix_in_kernel,
        grid=(B, S // tm),
        in_specs=[
            rows(D_MODEL), const((1, D_MODEL)), const((2 * D_ATT, D_MODEL)),
            const((D_MODEL, D_ATT)), const((D_MODEL, LANES)), const((1, LANES)),
            const((D_MODEL, 2 * D_GM)), const((tm, tm)),
            const((LANES, N_SPLIT * LANES)), const((N_SPLIT * LANES, LANES)),
            const((LANES, 1)), const((1, LANES)), const((1, D_GM)),
            const((N_PAIRS, 2 * CHUNK, CHUNK)), const((CHUNK, D_GM)), const((1, D_GM)),
        ],
        out_specs=[
            cols(D_ATT), cols(LANES), rows(D_ATT), rows(LANES),
            pl.BlockSpec((1, tm // KC, D_ATT, KC), lambda b, s: (b, s, 0, 0)),
            rows(D_GM),
        ],
        out_shape=[
            jax.ShapeDtypeStruct((B, D_ATT, S), BF16),
            jax.ShapeDtypeStruct((B, LANES, S), BF16),
            jax.ShapeDtypeStruct((B, S, D_ATT), BF16),
            jax.ShapeDtypeStruct((B, S, LANES), BF16),
            jax.ShapeDtypeStruct((B, S // KC, D_ATT, KC), BF16),
            jax.ShapeDtypeStruct((B, S, D_GM), BF16),
        ],
        scratch_shapes=[pltpu.VMEM((1, LANES), F32), pltpu.VMEM((tm, D_GM), F32)],
        compiler_params=pltpu.CompilerParams(
            dimension_semantics=("arbitrary", "arbitrary"), vmem_limit_bytes=VMEM_LIMIT),
        name="mix_in",
    )(h, gain, wqv, wk, wf, bf, wg, tril, placeq, placek, constq, constk, vgain, ws, bs, ggm)


def _attn_kernel(qt_ref, xqt_ref, k_ref, xk_ref, vt_ref, o_ref, qh_ref, acc_ref, st_ref, p_ref):
    tq = qt_ref.shape[2]
    pair = pl.program_id(1)
    i = pl.program_id(2)
    qt = jnp.concatenate([qt_ref[0], xqt_ref[0]], axis=0)
    row = lax.broadcasted_iota(jnp.int32, (EXT, 1), 0)
    for hh in range(2):
        slot = LANES + N_BIAS * (2 * pair + hh)
        own = ((row >= hh * HEAD_DIM) & (row < (hh + 1) * HEAD_DIM)) | ((row >= slot) & (row < slot + N_BIAS))
        qh_ref[hh] = jnp.where(own, qt, jnp.zeros_like(qt))
    acc_ref[...] = jnp.zeros(acc_ref.shape, F32)
    p_ref[2:] = jnp.zeros((2,) + p_ref.shape[1:], BF16)
    ones = (lax.broadcasted_iota(jnp.int32, (ONES_ROWS, KC), 0) == 0).astype(BF16)

    def scores(c, slot, lo=0):
        start = pl.multiple_of(c * KC, KC)
        k = jnp.concatenate([k_ref[0, pl.ds(start, KC), :], xk_ref[0, pl.ds(start, KC), :]], axis=1)
        for hh in range(2):
            st_ref[slot, hh, :, lo:] = _dot(k, qh_ref[hh, :, lo:])

    def softmax(slot, m, lo=0, diagonal=False):
        m_out, alpha = [], []
        for hh in range(2):
            st = st_ref[slot, hh, :, lo:]
            if diagonal:
                key = lax.broadcasted_iota(jnp.int32, st.shape, 0)
                qry = lax.broadcasted_iota(jnp.int32, st.shape, 1)
                st = jnp.where(key <= qry, st, NEG_INF)
            m_prev = m[hh][:, lo:]
            m_next = jnp.maximum(m_prev, jnp.max(st, axis=0, keepdims=True))
            p_ref[slot, hh, :, lo:] = jnp.exp2(st - m_next).astype(BF16)
            alpha.append(jnp.exp2(m_prev - m_next))
            m_out.append(jnp.concatenate([m[hh][:, :lo], m_next], axis=1) if lo else m_next)
        return tuple(m_out), tuple(alpha)

    def pv(c, slot, alpha, lo=0):
        vt = vt_ref[0, c]
        for hh in range(2):
            vx = jnp.concatenate([vt[hh * HEAD_DIM:(hh + 1) * HEAD_DIM], ones], axis=0)
            acc_ref[hh, :, lo:] = alpha[hh] * acc_ref[hh, :, lo:] + _dot(vx, p_ref[slot, hh, :, lo:])

    def body(t, carry):
        m, pending = carry
        for s in (0, 2):
            c = N_SLOT * t + s
            o = (s + 2) % N_SLOT
            scores(c + 2, o)
            scores(c + 3, o + 1)
            m, a0 = softmax(s, m)
            m, a1 = softmax(s + 1, m)
            pv(jnp.maximum(c - 2, 0), o, pending[0])
            pv(jnp.maximum(c - 1, 0), o + 1, pending[1])
            pending = (a0, a1)
        return m, pending

    assert tq == N_SLOT * KC and N_SLOT == 4
    lowest = jnp.full((1, tq), NEG_INF, F32)
    one = jnp.ones((1, tq), F32)
    scores(0, 0)
    scores(1, 1)
    m, pending = lax.fori_loop(0, i, body, ((lowest, lowest), ((one, one), (one, one))))
    c = N_SLOT * i
    scores(c + 2, 2, 2 * KC)
    scores(c + 3, 3, 3 * KC)
    m, a0 = softmax(0, m, 0, True)
    m, a1 = softmax(1, m, KC, True)
    pv(jnp.maximum(c - 2, 0), 2, pending[0])
    pv(jnp.maximum(c - 1, 0), 3, pending[1])
    m, a2 = softmax(2, m, 2 * KC, True)
    m, a3 = softmax(3, m, 3 * KC, True)
    pv(c, 0, a0)
    pv(c + 1, 1, a1, KC)
    pv(c + 2, 2, a2, 2 * KC)
    pv(c + 3, 3, a3, 3 * KC)
    out_t = jnp.concatenate(
        [acc_ref[hh, :HEAD_DIM] / acc_ref[hh, HEAD_DIM:HEAD_DIM + 1] for hh in range(2)], axis=0)
    o_ref[0] = out_t.T


def _attn(qt, xqt, k, xk, vt):
    B, S, _ = k.shape
    return pl.pallas_call(
        _attn_kernel,
        grid=(B, N_PAIRS, S // TQ),
        in_specs=[
            pl.BlockSpec((1, LANES, TQ), lambda b, j, i: (b, j, i)),
            pl.BlockSpec((1, LANES, TQ), lambda b, j, i: (b, 0, i)),
            pl.BlockSpec((1, S, LANES), lambda b, j, i: (b, 0, j)),
            pl.BlockSpec((1, S, LANES), lambda b, j, i: (b, 0, 0)),
            pl.BlockSpec((1, S // KC, LANES, KC), lambda b, j, i: (b, 0, j, 0)),
        ],
        out_specs=pl.BlockSpec((1, TQ, LANES), lambda b, j, i: (b, i, j)),
        out_shape=jax.ShapeDtypeStruct((B, S, D_ATT), F32),
        scratch_shapes=[
            pltpu.VMEM((2, EXT, TQ), BF16),
            pltpu.VMEM((2, ACC_ROWS, TQ), F32),
            pltpu.VMEM((N_SLOT, 2, KC, TQ), F32),
            pltpu.VMEM((N_SLOT, 2, KC, TQ), BF16),
        ],
        compiler_params=pltpu.CompilerParams(
            dimension_semantics=("arbitrary", "arbitrary", "arbitrary"),
            vmem_limit_bytes=VMEM_LIMIT),
        name="attn",
    )(qt, xqt, k, xk, vt)


def _post_kernel(h_ref, att_ref, gm_ref, p_ref, gatt_ref, wout_ref, gpost_ref, gpre_ref,
                 wa_ref, wb_ref, wfo_ref, gfpost_ref, wple_ref, gple_ref, wgate_ref, o_ref):
    an = (_rms(att_ref[...]) * gatt_ref[...]).astype(BF16)
    mixed = jnp.concatenate([an, gm_ref[...]], axis=1)
    h1 = h_ref[...] + _rms(_dot(mixed, wout_ref[...])) * gpost_ref[...]
    hn = (_rms(h1) * gpre_ref[...]).astype(BF16)
    a = _dot(hn, wa_ref[...])
    b = _dot(hn, wb_ref[...])
    act = (jax.nn.silu(a) * b).astype(BF16)
    h2 = h1 + _rms(_dot(act, wfo_ref[...])) * gfpost_ref[...]
    e = _rms(_dot(p_ref[...].astype(BF16), wple_ref[...])) * gple_ref[...]
    gate = jax.nn.sigmoid(_dot(_rms(h2).astype(BF16), wgate_ref[...]))
    o_ref[...] = h2 + gate * e


def _post(h, att, gm, p, gatt, wout, gpost, gpre, wa, wb, wfo, gfpost, wple, gple, wgate):
    N = h.shape[0]
    tm = TM_POST
    const = lambda shape: pl.BlockSpec(shape, lambda r: (0, 0), pipeline_mode=pl.Buffered(1))
    tile = lambda w: pl.BlockSpec((tm, w), lambda r: (r, 0))
    return pl.pallas_call(
        _post_kernel,
        grid=(N // tm,),
        in_specs=[
            tile(D_MODEL), tile(D_ATT), tile(D_GM), tile(PLE_DIM),
            const((1, D_ATT)), const((D_MODEL, D_MODEL)), const((1, D_MODEL)), const((1, D_MODEL)),
            const((D_MODEL, D_FF)), const((D_MODEL, D_FF)), const((D_FF, D_MODEL)),
            const((1, D_MODEL)), const((PLE_DIM, D_MODEL)), const((1, D_MODEL)),
            const((D_MODEL, D_MODEL)),
        ],
        out_specs=tile(D_MODEL),
        out_shape=jax.ShapeDtypeStruct((N, D_MODEL), F32),
        compiler_params=pltpu.CompilerParams(
            dimension_semantics=("arbitrary",), vmem_limit_bytes=VMEM_LIMIT),
        name="post",
    )(h, att, gm, p, gatt, wout, gpost, gpre, wa, wb, wfo, gfpost, wple, gple, wgate)


def _placement():
    placeq = np.zeros((LANES, N_SPLIT * LANES), np.float32)
    placek = np.zeros((N_SPLIT * LANES, LANES), np.float32)
    constq = np.zeros((LANES, 1), np.float32)
    constk = np.zeros((1, LANES), np.float32)
    for h in range(N_HEADS):
        for t in range(N_SPLIT):
            placeq[N_BIAS * h + t, t * LANES + h] = 1.0
            placek[t * LANES + h, N_BIAS * h + N_SPLIT + t] = -1.0
            constq[N_BIAS * h + N_SPLIT + t, 0] = 1.0
            constk[0, N_BIAS * h + t] = 1.0
    return placeq, placek, constq, constk


def kernel(x, p, mix_pre_norm, mix_post_norm, w_in, b_forget, gm_v_norm, gm_w_s, gm_b_s,
           mix_out_norm, w_out, ffn_pre_norm, ffn_post_norm, w_ffn_in, w_ffn_out, w_ple,
           ple_norm, w_ple_gate):
    B, S, D = x.shape
    depth = w_in.shape[0]
    placeq, placek, constq, constk = _placement()
    placeq = jnp.asarray(placeq, BF16)
    placek = jnp.asarray(placek, BF16)
    constq = jnp.asarray(constq)
    constk = jnp.asarray(constk)
    tril = jnp.asarray(np.tril(np.ones((TM_IN, TM_IN), np.float32)), BF16)
    row = lambda g: g.reshape(1, -1)

    h = x
    for i in range(depth):
        wi = w_in[i]
        wqv = jnp.concatenate([wi[:, :D_ATT], wi[:, 2 * D_ATT:3 * D_ATT]], axis=1).T.astype(BF16)
        wk = wi[:, D_ATT:2 * D_ATT].astype(BF16)
        wf = jnp.pad(wi[:, 3 * D_ATT:3 * D_ATT + N_HEADS], ((0, 0), (0, LANES - N_HEADS))).astype(BF16)
        wg = wi[:, 3 * D_ATT + N_HEADS:].astype(BF16)
        bf = jnp.pad(b_forget[i], (0, LANES - N_HEADS)).reshape(1, LANES)
        ws = gm_w_s[i].reshape(N_PAIRS, 2 * CHUNK, CHUNK)
        bs = jnp.repeat(gm_b_s[i].T, GROUP_DIM, axis=1)
        qt, xqt, k, xk, vt, gm = _mix_in(
            h, row(mix_pre_norm[i]), wqv, wk, wf, bf, wg, tril, placeq, placek, constq, constk,
            row(gm_v_norm[i]), ws, bs, row(mix_out_norm[i, D_ATT:]))
        att = _attn(qt, xqt, k, xk, vt)
        h = _post(
            h.reshape(B * S, D), att.reshape(B * S, D_ATT), gm.reshape(B * S, D_GM),
            p[i].reshape(B * S, PLE_DIM), row(mix_out_norm[i, :D_ATT]), w_out[i].astype(BF16),
            row(mix_post_norm[i]), row(ffn_pre_norm[i]), w_ffn_in[i, :, :D_FF].astype(BF16),
            w_ffn_in[i, :, D_FF:].astype(BF16), w_ffn_out[i].astype(BF16), row(ffn_post_norm[i]),
            w_ple[i].astype(BF16), row(ple_norm[i]), w_ple_gate[i].astype(BF16),
        ).reshape(B, S, D)
    return h
```

```python
import numpy as np
import jax
import jax.numpy as jnp
from jax import lax
from jax.experimental import pallas as pl
from jax.experimental.pallas import tpu as pltpu

D_MODEL = 1024
HEAD_DIM = 64
N_HEADS = 8
D_ATT = N_HEADS * HEAD_DIM
N_PAIRS = N_HEADS // 2
LANES = 128
N_GROUPS = 8
GROUP_DIM = 64
D_GM = N_GROUPS * GROUP_DIM
CHUNK = 128
D_FF = 2816
PLE_DIM = 256
EPS = 1e-6
NEG_INF = -1e30
LOG2E = 1.4426950408889634
N_SPLIT = 3
N_BIAS = 2 * N_SPLIT
EXT = 2 * LANES

TM_IN = 512
TQ = 1024
KC = 256
N_SLOT = TQ // KC
ONES_ROWS = 16
ACC_ROWS = HEAD_DIM + ONES_ROWS
TM_POST = 512
SUB_POST = 256
VMEM_LIMIT = 56 * 1024 * 1024

F32 = jnp.float32
BF16 = jnp.bfloat16
NT = (((1,), (1,)), ((), ()))


def _rms(x):
    return x * lax.rsqrt(jnp.mean(x * x, axis=-1, keepdims=True) + EPS)


def _split3(x):
    hi = x.astype(BF16)
    r1 = x - hi.astype(F32)
    mid = r1.astype(BF16)
    lo = (r1 - mid.astype(F32)).astype(BF16)
    return jnp.concatenate([hi, mid, lo], axis=1)


def _dot(a, b):
    return jnp.dot(a, b, preferred_element_type=F32)


def _dot_nt(a, b):
    return lax.dot_general(a, b, NT, preferred_element_type=F32)


def _mix_in_kernel(h_ref, gain_ref, wqv_ref, wk_ref, wf_ref, bf_ref, wg_ref, tril_ref,
                   placeq_ref, placek_ref, constq_ref, constk_ref, vgain_ref, ws_ref, bs_ref,
                   ggm_ref, qt_ref, xqt_ref, k_ref, xk_ref, vt_ref, gm_ref, carry_ref, gm_scr):
    tm = h_ref.shape[1]

    @pl.when(pl.program_id(1) == 0)
    def _():
        carry_ref[...] = jnp.zeros_like(carry_ref)

    hb = (_rms(h_ref[0]) * gain_ref[...]).astype(BF16)
    ge = jax.nn.gelu(_dot(hb, wg_ref[...]))

    zf = _dot(hb, wf_ref[...]) + bf_ref[...]
    logf = jnp.minimum(zf, 0.0) - jnp.log1p(jnp.exp(-jnp.abs(zf)))
    lane = lax.broadcasted_iota(jnp.int32, (1, LANES), 1)
    logf = jnp.where(lane < N_HEADS, logf, 0.0)
    cs = _dot(tril_ref[...], _split3(logf))
    c = cs[:, :LANES] + cs[:, LANES:2 * LANES] + cs[:, 2 * LANES:] + carry_ref[...]
    carry_ref[...] = c[tm - 1:tm, :]
    c3 = _split3(c * LOG2E)
    xqt_ref[0] = (_dot_nt(placeq_ref[...], c3) + constq_ref[...]).astype(BF16)
    xk_ref[0] = (_dot(c3, placek_ref[...]) + constk_ref[...]).astype(BF16)

    zqv = _dot_nt(wqv_ref[...], hb)
    qt_ref[0] = (zqv[:D_ATT] * (HEAD_DIM ** -0.5 * LOG2E)).astype(BF16)
    vt = zqv[D_ATT:].astype(BF16)
    for n in range(tm // KC):
        vt_ref[0, n] = vt[:, n * KC:(n + 1) * KC]
    k_ref[0] = _dot(hb, wk_ref[...]).astype(BF16)

    first = lane < GROUP_DIM
    row = lax.broadcasted_iota(jnp.int32, (2 * CHUNK, CHUNK), 0)
    col = lax.broadcasted_iota(jnp.int32, (2 * CHUNK, CHUNK), 1)
    causal = (row % CHUNK) >= col
    for j in range(N_PAIRS):
        pair = slice(j * LANES, (j + 1) * LANES)
        u = ge[:, pair]
        x = ge[:, D_GM + j * LANES:D_GM + (j + 1) * LANES]
        s_a = jnp.sum(jnp.where(first, x, 0.0), axis=-1, keepdims=True)
        s_b = jnp.sum(jnp.where(first, 0.0, x), axis=-1, keepdims=True)
        d = x - jnp.where(first, s_a, s_b) * (1.0 / GROUP_DIM)
        d2 = d * d
        v_a = jnp.sum(jnp.where(first, d2, 0.0), axis=-1, keepdims=True)
        v_b = jnp.sum(jnp.where(first, 0.0, d2), axis=-1, keepdims=True)
        var = jnp.where(first, v_a, v_b) * (1.0 / GROUP_DIM)
        vn = (d * lax.rsqrt(var + EPS) * vgain_ref[:, pair]).astype(BF16)
        w2 = jnp.where(causal, ws_ref[j], 0.0).astype(BF16)
        for n in range(tm // CHUNK):
            rows = slice(n * CHUNK, (n + 1) * CHUNK)
            mm = _dot(w2, vn[rows])
            mixed = jnp.where(first, mm[:CHUNK], mm[CHUNK:]) + bs_ref[:, pair]
            gm_scr[rows, pair] = u[rows] * mixed
    gm_ref[0] = (_rms(gm_scr[...]) * ggm_ref[...]).astype(BF16)


def _mix_in(layer, h, gain, wqv, wk, wf, bf, wg, tril, placeq, placek, constq, constk, vgain, ws, bs, gout):
    B, S, _ = h.shape
    tm = TM_IN
    const = lambda shape: pl.BlockSpec(shape, lambda b, s: (0,) * len(shape))
    param = lambda *shape: pl.BlockSpec((None,) + shape, lambda b, s: (layer,) + (0,) * len(shape))
    rows = lambda w: pl.BlockSpec((1, tm, w), lambda b, s: (b, s, 0))
    cols = lambda w: pl.BlockSpec((1, w, tm), lambda b, s: (b, 0, s))
    return pl.pallas_call(
        _mix_in_kernel,
        grid=(B, S // tm),
        in_specs=[
            rows(D_MODEL), param(1, D_MODEL), param(2 * D_ATT, D_MODEL),
            param(D_MODEL, D_ATT), param(D_MODEL, LANES), param(1, LANES),
            param(D_MODEL, 2 * D_GM), const((tm, tm)),
            const((LANES, N_SPLIT * LANES)), const((N_SPLIT * LANES, LANES)),
            const((LANES, 1)), const((1, LANES)), param(1, D_GM),
            param(N_PAIRS, 2 * CHUNK, CHUNK), param(CHUNK, D_GM),
            pl.BlockSpec((None, 1, D_GM), lambda b, s: (layer, 0, 1)),
        ],
        out_specs=[
            cols(D_ATT), cols(LANES), rows(D_ATT), rows(LANES),
            pl.BlockSpec((1, tm // KC, D_ATT, KC), lambda b, s: (b, s, 0, 0)),
            rows(D_GM),
        ],
        out_shape=[
            jax.ShapeDtypeStruct((B, D_ATT, S), BF16),
            jax.ShapeDtypeStruct((B, LANES, S), BF16),
            jax.ShapeDtypeStruct((B, S, D_ATT), BF16),
            jax.ShapeDtypeStruct((B, S, LANES), BF16),
            jax.ShapeDtypeStruct((B, S // KC, D_ATT, KC), BF16),
            jax.ShapeDtypeStruct((B, S, D_GM), BF16),
        ],
        scratch_shapes=[pltpu.VMEM((1, LANES), F32), pltpu.VMEM((tm, D_GM), F32)],
        compiler_params=pltpu.CompilerParams(
            dimension_semantics=("arbitrary", "arbitrary"), vmem_limit_bytes=VMEM_LIMIT),
        name="mix_in",
    )(h, gain, wqv, wk, wf, bf, wg, tril, placeq, placek, constq, constk, vgain, ws, bs, gout)


def _attn_kernel(qt_ref, xqt_ref, k_ref, xk_ref, vt_ref, o_ref, qh_ref, acc_ref, st_ref, p_ref):
    tq = qt_ref.shape[2]
    pair = pl.program_id(1)
    i = pl.program_id(2)
    qt = jnp.concatenate([qt_ref[0], xqt_ref[0]], axis=0)
    row = lax.broadcasted_iota(jnp.int32, (EXT, 1), 0)
    for hh in range(2):
        slot = LANES + N_BIAS * (2 * pair + hh)
        own = ((row >= hh * HEAD_DIM) & (row < (hh + 1) * HEAD_DIM)) | ((row >= slot) & (row < slot + N_BIAS))
        qh_ref[hh] = jnp.where(own, qt, jnp.zeros_like(qt))
    acc_ref[...] = jnp.zeros(acc_ref.shape, F32)
    p_ref[2:] = jnp.zeros((2,) + p_ref.shape[1:], BF16)
    ones = (lax.broadcasted_iota(jnp.int32, (ONES_ROWS, KC), 0) == 0).astype(BF16)

    def scores(c, slot, lo=0):
        start = pl.multiple_of(c * KC, KC)
        k = jnp.concatenate([k_ref[0, pl.ds(start, KC), :], xk_ref[0, pl.ds(start, KC), :]], axis=1)
        for hh in range(2):
            st_ref[slot, hh, :, lo:] = _dot(k, qh_ref[hh, :, lo:])

    def softmax(slot, m, lo=0, diagonal=False):
        m_out, alpha = [], []
        for hh in range(2):
            st = st_ref[slot, hh, :, lo:]
            if diagonal:
                key = lax.broadcasted_iota(jnp.int32, st.shape, 0)
                qry = lax.broadcasted_iota(jnp.int32, st.shape, 1)
                st = jnp.where(key <= qry, st, NEG_INF)
            m_prev = m[hh][:, lo:]
            m_next = jnp.maximum(m_prev, jnp.max(st, axis=0, keepdims=True))
            p_ref[slot, hh, :, lo:] = jnp.exp2(st - m_next).astype(BF16)
            alpha.append(jnp.exp2(m_prev - m_next))
            m_out.append(jnp.concatenate([m[hh][:, :lo], m_next], axis=1) if lo else m_next)
        return tuple(m_out), tuple(alpha)

    def pv(c, slot, alpha, lo=0):
        vt = vt_ref[0, c]
        for hh in range(2):
            vx = jnp.concatenate([vt[hh * HEAD_DIM:(hh + 1) * HEAD_DIM], ones], axis=0)
            acc_ref[hh, :, lo:] = alpha[hh] * acc_ref[hh, :, lo:] + _dot(vx, p_ref[slot, hh, :, lo:])

    def body(t, carry):
        m, pending = carry
        for s in (0, 2):
            c = N_SLOT * t + s
            o = (s + 2) % N_SLOT
            scores(c + 2, o)
            scores(c + 3, o + 1)
            m, a0 = softmax(s, m)
            m, a1 = softmax(s + 1, m)
            pv(jnp.maximum(c - 2, 0), o, pending[0])
            pv(jnp.maximum(c - 1, 0), o + 1, pending[1])
            pending = (a0, a1)
        return m, pending

    assert tq == N_SLOT * KC and N_SLOT == 4
    lowest = jnp.full((1, tq), NEG_INF, F32)
    one = jnp.ones((1, tq), F32)
    scores(0, 0)
    scores(1, 1)
    m, pending = lax.fori_loop(0, i, body, ((lowest, lowest), ((one, one), (one, one))))
    c = N_SLOT * i
    scores(c + 2, 2, 2 * KC)
    scores(c + 3, 3, 3 * KC)
    m, a0 = softmax(0, m, 0, True)
    m, a1 = softmax(1, m, KC, True)
    pv(jnp.maximum(c - 2, 0), 2, pending[0])
    pv(jnp.maximum(c - 1, 0), 3, pending[1])
    m, a2 = softmax(2, m, 2 * KC, True)
    m, a3 = softmax(3, m, 3 * KC, True)
    pv(c, 0, a0)
    pv(c + 1, 1, a1, KC)
    pv(c + 2, 2, a2, 2 * KC)
    pv(c + 3, 3, a3, 3 * KC)
    out_t = jnp.concatenate(
        [acc_ref[hh, :HEAD_DIM] / acc_ref[hh, HEAD_DIM:HEAD_DIM + 1] for hh in range(2)], axis=0)
    o_ref[0] = out_t.T


def _attn(qt, xqt, k, xk, vt):
    B, S, _ = k.shape
    return pl.pallas_call(
        _attn_kernel,
        grid=(B, N_PAIRS, S // TQ),
        in_specs=[
            pl.BlockSpec((1, LANES, TQ), lambda b, j, i: (b, j, i)),
            pl.BlockSpec((1, LANES, TQ), lambda b, j, i: (b, 0, i)),
            pl.BlockSpec((1, S, LANES), lambda b, j, i: (b, 0, j)),
            pl.BlockSpec((1, S, LANES), lambda b, j, i: (b, 0, 0)),
            pl.BlockSpec((1, S // KC, LANES, KC), lambda b, j, i: (b, 0, j, 0)),
        ],
        out_specs=pl.BlockSpec((1, TQ, LANES), lambda b, j, i: (b, i, j)),
        out_shape=jax.ShapeDtypeStruct((B, S, D_ATT), F32),
        scratch_shapes=[
            pltpu.VMEM((2, EXT, TQ), BF16),
            pltpu.VMEM((2, ACC_ROWS, TQ), F32),
            pltpu.VMEM((N_SLOT, 2, KC, TQ), F32),
            pltpu.VMEM((N_SLOT, 2, KC, TQ), BF16),
        ],
        compiler_params=pltpu.CompilerParams(
            dimension_semantics=("arbitrary", "arbitrary", "arbitrary"),
            vmem_limit_bytes=VMEM_LIMIT),
        name="attn",
    )(qt, xqt, k, xk, vt)


def _post_kernel(h_ref, att_ref, gm_ref, p_ref, gatt_ref, wout_ref, gpost_ref, gpre_ref,
                 wa_ref, wb_ref, wfo_ref, gfpost_ref, wple_ref, gple_ref, wgate_ref, o_ref):
    subs = [slice(r * SUB_POST, (r + 1) * SUB_POST) for r in range(h_ref.shape[0] // SUB_POST)]
    e = [_rms(_dot(p_ref[s].astype(BF16), wple_ref[...])) * gple_ref[...] for s in subs]
    mixed = [jnp.concatenate([(_rms(att_ref[s]) * gatt_ref[...]).astype(BF16), gm_ref[s]], axis=1)
             for s in subs]
    y = [_dot(x, wout_ref[...]) for x in mixed]
    h1 = [h_ref[s] + _rms(t) * gpost_ref[...] for s, t in zip(subs, y)]
    hn = [(_rms(t) * gpre_ref[...]).astype(BF16) for t in h1]
    act = [(jax.nn.silu(_dot(t, wa_ref[...])) * _dot(t, wb_ref[...])).astype(BF16) for t in hn]
    y = [_dot(t, wfo_ref[...]) for t in act]
    h2 = [t + _rms(u) * gfpost_ref[...] for t, u in zip(h1, y)]
    gate = [jax.nn.sigmoid(_dot(_rms(t).astype(BF16), wgate_ref[...])) for t in h2]
    for s, t, g, u in zip(subs, h2, gate, e):
        o_ref[s] = t + g * u


def _post(layer, h, att, gm, p, gout, wout, gpost, gpre, wffn, wfo, gfpost, wple, gple, wgate):
    N = h.shape[0]
    tm = TM_POST
    param = lambda *shape, col=0: pl.BlockSpec(
        (None,) + shape, lambda r: (layer, 0, col), pipeline_mode=pl.Buffered(1))
    tile = lambda w: pl.BlockSpec((tm, w), lambda r: (r, 0))
    return pl.pallas_call(
        _post_kernel,
        grid=(N // tm,),
        in_specs=[
            tile(D_MODEL), tile(D_ATT), tile(D_GM),
            pl.BlockSpec((None, tm, PLE_DIM), lambda r: (layer, r, 0)),
            param(1, D_ATT), param(D_MODEL, D_MODEL), param(1, D_MODEL), param(1, D_MODEL),
            param(D_MODEL, D_FF), param(D_MODEL, D_FF, col=1), param(D_FF, D_MODEL),
            param(1, D_MODEL), param(PLE_DIM, D_MODEL), param(1, D_MODEL),
            param(D_MODEL, D_MODEL),
        ],
        out_specs=tile(D_MODEL),
        out_shape=jax.ShapeDtypeStruct((N, D_MODEL), F32),
        compiler_params=pltpu.CompilerParams(
            dimension_semantics=("arbitrary",), vmem_limit_bytes=VMEM_LIMIT),
        name="post",
    )(h, att, gm, p, gout, wout, gpost, gpre, wffn, wffn, wfo, gfpost, wple, gple, wgate)


def _placement():
    placeq = np.zeros((LANES, N_SPLIT * LANES), np.float32)
    placek = np.zeros((N_SPLIT * LANES, LANES), np.float32)
    constq = np.zeros((LANES, 1), np.float32)
    constk = np.zeros((1, LANES), np.float32)
    for h in range(N_HEADS):
        for t in range(N_SPLIT):
            placeq[N_BIAS * h + t, t * LANES + h] = 1.0
            placek[t * LANES + h, N_BIAS * h + N_SPLIT + t] = -1.0
            constq[N_BIAS * h + N_SPLIT + t, 0] = 1.0
            constk[0, N_BIAS * h + t] = 1.0
    return placeq, placek, constq, constk


def kernel(x, p, mix_pre_norm, mix_post_norm, w_in, b_forget, gm_v_norm, gm_w_s, gm_b_s,
           mix_out_norm, w_out, ffn_pre_norm, ffn_post_norm, w_ffn_in, w_ffn_out, w_ple,
           ple_norm, w_ple_gate):
    B, S, D = x.shape
    depth = w_in.shape[0]
    placeq, placek, constq, constk = _placement()
    placeq = jnp.asarray(placeq, BF16)
    placek = jnp.asarray(placek, BF16)
    constq = jnp.asarray(constq)
    constk = jnp.asarray(constk)
    tril = jnp.asarray(np.tril(np.ones((TM_IN, TM_IN), np.float32)), BF16)

    rows = lambda g: g.reshape(depth, 1, -1)
    wqv = jnp.concatenate([w_in[:, :, :D_ATT], w_in[:, :, 2 * D_ATT:3 * D_ATT]], axis=2)
    wqv = wqv.astype(BF16).transpose(0, 2, 1)
    wk = w_in[:, :, D_ATT:2 * D_ATT].astype(BF16)
    wf = jnp.pad(w_in[:, :, 3 * D_ATT:3 * D_ATT + N_HEADS].astype(BF16),
                 ((0, 0), (0, 0), (0, LANES - N_HEADS)))
    wg = w_in[:, :, 3 * D_ATT + N_HEADS:].astype(BF16)
    bf = rows(jnp.pad(b_forget, ((0, 0), (0, LANES - N_HEADS))))
    ws = gm_w_s.reshape(depth, N_PAIRS, 2 * CHUNK, CHUNK)
    bs = jnp.repeat(gm_b_s.transpose(0, 2, 1), GROUP_DIM, axis=2)
    gout = rows(mix_out_norm)
    wout, wffn, wfo = w_out.astype(BF16), w_ffn_in.astype(BF16), w_ffn_out.astype(BF16)
    wple, wgate = w_ple.astype(BF16), w_ple_gate.astype(BF16)
    p = p.reshape(depth, B * S, PLE_DIM)

    h = x
    for i in range(depth):
        qt, xqt, k, xk, vt, gm = _mix_in(
            i, h, rows(mix_pre_norm), wqv, wk, wf, bf, wg, tril, placeq, placek, constq, constk,
            rows(gm_v_norm), ws, bs, gout)
        att = _attn(qt, xqt, k, xk, vt)
        h = _post(
            i, h.reshape(B * S, D), att.reshape(B * S, D_ATT), gm.reshape(B * S, D_GM), p, gout,
            wout, rows(mix_post_norm), rows(ffn_pre_norm), wffn, wfo, rows(ffn_post_norm),
            wple, rows(ple_norm), wgate,
        ).reshape(B, S, D)
    return h
```

```python
import numpy as np
import jax
import jax.numpy as jnp
from jax import lax
from jax.experimental import pallas as pl
from jax.experimental.pallas import tpu as pltpu

D_MODEL = 1024
HEAD_DIM = 64
N_HEADS = 8
D_ATT = N_HEADS * HEAD_DIM
N_PAIRS = N_HEADS // 2
LANES = 128
N_GROUPS = 8
GROUP_DIM = 64
D_GM = N_GROUPS * GROUP_DIM
CHUNK = 128
D_FF = 2816
PLE_DIM = 256
EPS = 1e-6
NEG_INF = -1e30
LOG2E = 1.4426950408889634
N_SPLIT = 3
N_BIAS = 2 * N_SPLIT
EXT = 2 * LANES

TM_IN = 512
TQ = 1024
KC = 256
N_SLOT = TQ // KC
ONES_ROWS = 16
ACC_ROWS = HEAD_DIM + ONES_ROWS
TM_POST = 512
SUB_POST = 256
VMEM_LIMIT = 56 * 1024 * 1024

F32 = jnp.float32
BF16 = jnp.bfloat16
NT = (((1,), (1,)), ((), ()))


def _rms(x):
    return x * lax.rsqrt(jnp.mean(x * x, axis=-1, keepdims=True) + EPS)


def _split3(x):
    hi = x.astype(BF16)
    r1 = x - hi.astype(F32)
    mid = r1.astype(BF16)
    lo = (r1 - mid.astype(F32)).astype(BF16)
    return jnp.concatenate([hi, mid, lo], axis=1)


def _dot(a, b):
    return jnp.dot(a, b, preferred_element_type=F32)


def _dot_nt(a, b):
    return lax.dot_general(a, b, NT, preferred_element_type=F32)


def _mix_in_kernel(h_ref, gain_ref, wqv_ref, wk_ref, wf_ref, bf_ref, wg_ref, tril_ref,
                   placeq_ref, placek_ref, constq_ref, constk_ref, vgain_ref, ws_ref, bs_ref,
                   ggm_ref, qt_ref, xqt_ref, k_ref, xk_ref, vt_ref, gm_ref, carry_ref, gm_scr):
    tm = h_ref.shape[1]

    @pl.when(pl.program_id(1) == 0)
    def _():
        carry_ref[...] = jnp.zeros_like(carry_ref)

    hb = (_rms(h_ref[0]) * gain_ref[...]).astype(BF16)
    ge = jax.nn.gelu(_dot(hb, wg_ref[...]))

    zf = _dot(hb, wf_ref[...]) + bf_ref[...]
    logf = jnp.minimum(zf, 0.0) - jnp.log1p(jnp.exp(-jnp.abs(zf)))
    lane = lax.broadcasted_iota(jnp.int32, (1, LANES), 1)
    logf = jnp.where(lane < N_HEADS, logf, 0.0)
    cs = _dot(tril_ref[...], _split3(logf))
    c = cs[:, :LANES] + cs[:, LANES:2 * LANES] + cs[:, 2 * LANES:] + carry_ref[...]
    carry_ref[...] = c[tm - 1:tm, :]
    c3 = _split3(c * LOG2E)
    xqt_ref[0] = (_dot_nt(placeq_ref[...], c3) + constq_ref[...]).astype(BF16)
    xk_ref[0] = (_dot(c3, placek_ref[...]) + constk_ref[...]).astype(BF16)

    zqv = _dot_nt(wqv_ref[...], hb)
    qt_ref[0] = (zqv[:D_ATT] * (HEAD_DIM ** -0.5 * LOG2E)).astype(BF16)
    vt = zqv[D_ATT:].astype(BF16)
    for n in range(tm // KC):
        vt_ref[0, n] = vt[:, n * KC:(n + 1) * KC]
    k_ref[0] = _dot(hb, wk_ref[...]).astype(BF16)

    first = lane < GROUP_DIM
    row = lax.broadcasted_iota(jnp.int32, (2 * CHUNK, CHUNK), 0)
    col = lax.broadcasted_iota(jnp.int32, (2 * CHUNK, CHUNK), 1)
    causal = (row % CHUNK) >= col
    for j in range(N_PAIRS):
        pair = slice(j * LANES, (j + 1) * LANES)
        u = ge[:, pair]
        x = ge[:, D_GM + j * LANES:D_GM + (j + 1) * LANES]
        s_a = jnp.sum(jnp.where(first, x, 0.0), axis=-1, keepdims=True)
        s_b = jnp.sum(jnp.where(first, 0.0, x), axis=-1, keepdims=True)
        d = x - jnp.where(first, s_a, s_b) * (1.0 / GROUP_DIM)
        d2 = d * d
        v_a = jnp.sum(jnp.where(first, d2, 0.0), axis=-1, keepdims=True)
        v_b = jnp.sum(jnp.where(first, 0.0, d2), axis=-1, keepdims=True)
        var = jnp.where(first, v_a, v_b) * (1.0 / GROUP_DIM)
        vn = (d * lax.rsqrt(var + EPS) * vgain_ref[:, pair]).astype(BF16)
        w2 = jnp.where(causal, ws_ref[j], 0.0).astype(BF16)
        for n in range(tm // CHUNK):
            rows = slice(n * CHUNK, (n + 1) * CHUNK)
            mm = _dot(w2, vn[rows])
            mixed = jnp.where(first, mm[:CHUNK], mm[CHUNK:]) + bs_ref[:, pair]
            gm_scr[rows, pair] = u[rows] * mixed
    gm_ref[0] = (_rms(gm_scr[...]) * ggm_ref[...]).astype(BF16)


def _mix_in(layer, h, gain, wqv, wk, wf, bf, wg, tril, placeq, placek, constq, constk, vgain, ws, bs, gout):
    B, S, _ = h.shape
    tm = TM_IN
    const = lambda shape: pl.BlockSpec(shape, lambda b, s: (0,) * len(shape))
    param = lambda *shape: pl.BlockSpec((None,) + shape, lambda b, s: (layer,) + (0,) * len(shape))
    rows = lambda w: pl.BlockSpec((1, tm, w), lambda b, s: (b, s, 0))
    cols = lambda w: pl.BlockSpec((1, w, tm), lambda b, s: (b, 0, s))
    return pl.pallas_call(
        _mix_in_kernel,
        grid=(B, S // tm),
        in_specs=[
            rows(D_MODEL), param(1, D_MODEL), param(2 * D_ATT, D_MODEL),
            param(D_MODEL, D_ATT), param(D_MODEL, LANES), param(1, LANES),
            param(D_MODEL, 2 * D_GM), const((tm, tm)),
            const((LANES, N_SPLIT * LANES)), const((N_SPLIT * LANES, LANES)),
            const((LANES, 1)), const((1, LANES)), param(1, D_GM),
            param(N_PAIRS, 2 * CHUNK, CHUNK), param(CHUNK, D_GM),
            pl.BlockSpec((None, 1, D_GM), lambda b, s: (layer, 0, 1)),
        ],
        out_specs=[
            cols(D_ATT), cols(LANES), rows(D_ATT), rows(LANES),
            pl.BlockSpec((1, tm // KC, D_ATT, KC), lambda b, s: (b, s, 0, 0)),
            rows(D_GM),
        ],
        out_shape=[
            jax.ShapeDtypeStruct((B, D_ATT, S), BF16),
            jax.ShapeDtypeStruct((B, LANES, S), BF16),
            jax.ShapeDtypeStruct((B, S, D_ATT), BF16),
            jax.ShapeDtypeStruct((B, S, LANES), BF16),
            jax.ShapeDtypeStruct((B, S // KC, D_ATT, KC), BF16),
            jax.ShapeDtypeStruct((B, S, D_GM), BF16),
        ],
        scratch_shapes=[pltpu.VMEM((1, LANES), F32), pltpu.VMEM((tm, D_GM), F32)],
        compiler_params=pltpu.CompilerParams(
            dimension_semantics=("arbitrary", "arbitrary"), vmem_limit_bytes=VMEM_LIMIT),
        name="mix_in",
    )(h, gain, wqv, wk, wf, bf, wg, tril, placeq, placek, constq, constk, vgain, ws, bs, gout)


def _attn_kernel(qt_ref, xqt_ref, k_ref, xk_ref, vt_ref, o_ref, qh_ref, acc_ref, st_ref, p_ref):
    pair = pl.program_id(1)
    row = lax.broadcasted_iota(jnp.int32, (EXT, 1), 0)
    own = []
    for hh in range(2):
        slot = LANES + N_BIAS * (2 * pair + hh)
        own.append(((row >= hh * HEAD_DIM) & (row < (hh + 1) * HEAD_DIM)) | ((row >= slot) & (row < slot + N_BIAS)))
    ones = (lax.broadcasted_iota(jnp.int32, (ONES_ROWS, KC), 0) == 0).astype(BF16)

    @pl.when((pl.program_id(0) == 0) & (pair == 0))
    def _():
        p_ref[2:] = jnp.zeros((2,) + p_ref.shape[1:], BF16)

    for i in range(qt_ref.shape[2] // TQ):
        _attn_tile(i, own, ones, qt_ref, xqt_ref, k_ref, xk_ref, vt_ref, o_ref,
                   qh_ref.at[i % 2], acc_ref.at[i % 2], st_ref, p_ref)


def _attn_tile(i, own, ones, qt_ref, xqt_ref, k_ref, xk_ref, vt_ref, o_ref, qh_ref, acc_ref, st_ref, p_ref):
    tq = TQ
    cols = slice(i * tq, (i + 1) * tq)
    qt = jnp.concatenate([qt_ref[0, :, cols], xqt_ref[0, :, cols]], axis=0)
    for hh in range(2):
        qh_ref[hh] = jnp.where(own[hh], qt, jnp.zeros_like(qt))
    acc_ref[...] = jnp.zeros(acc_ref.shape, F32)

    def scores(c, slot, lo=0):
        start = c * KC if isinstance(c, int) else pl.multiple_of(c * KC, KC)
        k = jnp.concatenate([k_ref[0, pl.ds(start, KC), :], xk_ref[0, pl.ds(start, KC), :]], axis=1)
        for hh in range(2):
            st_ref[slot, hh, :, lo:] = _dot(k, qh_ref[hh, :, lo:])

    def softmax(slot, m, lo=0, diagonal=False):
        m_out, alpha = [], []
        for hh in range(2):
            st = st_ref[slot, hh, :, lo:]
            if diagonal:
                key = lax.broadcasted_iota(jnp.int32, st.shape, 0)
                qry = lax.broadcasted_iota(jnp.int32, st.shape, 1)
                st = jnp.where(key <= qry, st, NEG_INF)
            m_prev = m[hh][:, lo:]
            m_next = jnp.maximum(m_prev, jnp.max(st, axis=0, keepdims=True))
            p_ref[slot, hh, :, lo:] = jnp.exp2(st - m_next).astype(BF16)
            alpha.append(jnp.exp2(m_prev - m_next))
            m_out.append(jnp.concatenate([m[hh][:, :lo], m_next], axis=1) if lo else m_next)
        return tuple(m_out), tuple(alpha)

    def pv(c, slot, alpha, lo=0, live=None):
        vt = vt_ref[0, c]
        for hh in range(2):
            vx = jnp.concatenate([vt[hh * HEAD_DIM:(hh + 1) * HEAD_DIM], ones], axis=0)
            if live is not None:
                vx = jnp.where(live, vx, jnp.zeros_like(vx))
            acc_ref[hh, :, lo:] = alpha[hh] * acc_ref[hh, :, lo:] + _dot(vx, p_ref[slot, hh, :, lo:])

    def body(t, carry):
        m, pending = carry
        for s in (0, 2):
            c = N_SLOT * t + s
            o = (s + 2) % N_SLOT
            scores(c + 2, o)
            scores(c + 3, o + 1)
            m, a0 = softmax(s, m)
            m, a1 = softmax(s + 1, m)
            live = None if s else t > 0
            pv(jnp.maximum(c - 2, 0), o, pending[0], 0, live)
            pv(jnp.maximum(c - 1, 0), o + 1, pending[1], 0, live)
            pending = (a0, a1)
        return m, pending

    assert tq == N_SLOT * KC and N_SLOT == 4
    lowest = jnp.full((1, tq), NEG_INF, F32)
    one = jnp.ones((1, tq), F32)
    scores(0, 0)
    scores(1, 1)
    carry = ((lowest, lowest), ((one, one), (one, one)))
    m, pending = lax.fori_loop(0, i, body, carry) if i else carry
    c = N_SLOT * i
    scores(c + 2, 2, 2 * KC)
    scores(c + 3, 3, 3 * KC)
    m, a0 = softmax(0, m, 0, True)
    m, a1 = softmax(1, m, KC, True)
    if i:
        pv(c - 2, 2, pending[0])
        pv(c - 1, 3, pending[1])
    m, a2 = softmax(2, m, 2 * KC, True)
    m, a3 = softmax(3, m, 3 * KC, True)
    pv(c, 0, a0)
    pv(c + 1, 1, a1, KC)
    pv(c + 2, 2, a2, 2 * KC)
    pv(c + 3, 3, a3, 3 * KC)
    out_t = jnp.concatenate(
        [acc_ref[hh, :HEAD_DIM] / acc_ref[hh, HEAD_DIM:HEAD_DIM + 1] for hh in range(2)], axis=0)
    o_ref[0, cols, :] = out_t.T


def _attn(qt, xqt, k, xk, vt):
    B, S, _ = k.shape
    return pl.pallas_call(
        _attn_kernel,
        grid=(B, N_PAIRS),
        in_specs=[
            pl.BlockSpec((1, LANES, S), lambda b, j: (b, j, 0)),
            pl.BlockSpec((1, LANES, S), lambda b, j: (b, 0, 0)),
            pl.BlockSpec((1, S, LANES), lambda b, j: (b, 0, j)),
            pl.BlockSpec((1, S, LANES), lambda b, j: (b, 0, 0)),
            pl.BlockSpec((1, S // KC, LANES, KC), lambda b, j: (b, 0, j, 0)),
        ],
        out_specs=pl.BlockSpec((1, S, LANES), lambda b, j: (b, 0, j)),
        out_shape=jax.ShapeDtypeStruct((B, S, D_ATT), F32),
        scratch_shapes=[
            pltpu.VMEM((2, 2, EXT, TQ), BF16),
            pltpu.VMEM((2, 2, ACC_ROWS, TQ), F32),
            pltpu.VMEM((N_SLOT, 2, KC, TQ), F32),
            pltpu.VMEM((N_SLOT, 2, KC, TQ), BF16),
        ],
        compiler_params=pltpu.CompilerParams(
            dimension_semantics=("arbitrary", "arbitrary"), vmem_limit_bytes=VMEM_LIMIT),
        name="attn",
    )(qt, xqt, k, xk, vt)


def _post_kernel(h_ref, att_ref, gm_ref, p_ref, gatt_ref, wout_ref, gpost_ref, gpre_ref,
                 wa_ref, wb_ref, wfo_ref, gfpost_ref, wple_ref, gple_ref, wgate_ref, o_ref):
    subs = [slice(r * SUB_POST, (r + 1) * SUB_POST) for r in range(h_ref.shape[0] // SUB_POST)]
    e = [_rms(_dot(p_ref[s].astype(BF16), wple_ref[...])) * gple_ref[...] for s in subs]
    mixed = [jnp.concatenate([(_rms(att_ref[s]) * gatt_ref[...]).astype(BF16), gm_ref[s]], axis=1)
             for s in subs]
    y = [_dot(x, wout_ref[...]) for x in mixed]
    h1 = [h_ref[s] + _rms(t) * gpost_ref[...] for s, t in zip(subs, y)]
    hn = [(_rms(t) * gpre_ref[...]).astype(BF16) for t in h1]
    act = [(jax.nn.silu(_dot(t, wa_ref[...])) * _dot(t, wb_ref[...])).astype(BF16) for t in hn]
    y = [_dot(t, wfo_ref[...]) for t in act]
    h2 = [t + _rms(u) * gfpost_ref[...] for t, u in zip(h1, y)]
    gate = [jax.nn.sigmoid(_dot(_rms(t).astype(BF16), wgate_ref[...])) for t in h2]
    for s, t, g, u in zip(subs, h2, gate, e):
        o_ref[s] = t + g * u


def _post(layer, h, att, gm, p, gout, wout, gpost, gpre, wffn, wfo, gfpost, wple, gple, wgate):
    N = h.shape[0]
    tm = TM_POST
    param = lambda *shape, col=0: pl.BlockSpec(
        (None,) + shape, lambda r: (layer, 0, col), pipeline_mode=pl.Buffered(1))
    tile = lambda w: pl.BlockSpec((tm, w), lambda r: (r, 0))
    return pl.pallas_call(
        _post_kernel,
        grid=(N // tm,),
        in_specs=[
            tile(D_MODEL), tile(D_ATT), tile(D_GM),
            pl.BlockSpec((None, tm, PLE_DIM), lambda r: (layer, r, 0)),
            param(1, D_ATT), param(D_MODEL, D_MODEL), param(1, D_MODEL), param(1, D_MODEL),
            param(D_MODEL, D_FF), param(D_MODEL, D_FF, col=1), param(D_FF, D_MODEL),
            param(1, D_MODEL), param(PLE_DIM, D_MODEL), param(1, D_MODEL),
            param(D_MODEL, D_MODEL),
        ],
        out_specs=tile(D_MODEL),
        out_shape=jax.ShapeDtypeStruct((N, D_MODEL), F32),
        compiler_params=pltpu.CompilerParams(
            dimension_semantics=("arbitrary",), vmem_limit_bytes=VMEM_LIMIT),
        name="post",
    )(h, att, gm, p, gout, wout, gpost, gpre, wffn, wffn, wfo, gfpost, wple, gple, wgate)


def _placement():
    placeq = np.zeros((LANES, N_SPLIT * LANES), np.float32)
    placek = np.zeros((N_SPLIT * LANES, LANES), np.float32)
    constq = np.zeros((LANES, 1), np.float32)
    constk = np.zeros((1, LANES), np.float32)
    for h in range(N_HEADS):
        for t in range(N_SPLIT):
            placeq[N_BIAS * h + t, t * LANES + h] = 1.0
            placek[t * LANES + h, N_BIAS * h + N_SPLIT + t] = -1.0
            constq[N_BIAS * h + N_SPLIT + t, 0] = 1.0
            constk[0, N_BIAS * h + t] = 1.0
    return placeq, placek, constq, constk


def kernel(x, p, mix_pre_norm, mix_post_norm, w_in, b_forget, gm_v_norm, gm_w_s, gm_b_s,
           mix_out_norm, w_out, ffn_pre_norm, ffn_post_norm, w_ffn_in, w_ffn_out, w_ple,
           ple_norm, w_ple_gate):
    B, S, D = x.shape
    depth = w_in.shape[0]
    placeq, placek, constq, constk = _placement()
    placeq = jnp.asarray(placeq, BF16)
    placek = jnp.asarray(placek, BF16)
    constq = jnp.asarray(constq)
    constk = jnp.asarray(constk)
    tril = jnp.asarray(np.tril(np.ones((TM_IN, TM_IN), np.float32)), BF16)

    rows = lambda g: g.reshape(depth, 1, -1)
    wqv = jnp.concatenate([w_in[:, :, :D_ATT], w_in[:, :, 2 * D_ATT:3 * D_ATT]], axis=2)
    wqv = wqv.astype(BF16).transpose(0, 2, 1)
    wk = w_in[:, :, D_ATT:2 * D_ATT].astype(BF16)
    wf = jnp.pad(w_in[:, :, 3 * D_ATT:3 * D_ATT + N_HEADS].astype(BF16),
                 ((0, 0), (0, 0), (0, LANES - N_HEADS)))
    wg = w_in[:, :, 3 * D_ATT + N_HEADS:].astype(BF16)
    bf = rows(jnp.pad(b_forget, ((0, 0), (0, LANES - N_HEADS))))
    ws = gm_w_s.reshape(depth, N_PAIRS, 2 * CHUNK, CHUNK)
    bs = jnp.repeat(gm_b_s.transpose(0, 2, 1), GROUP_DIM, axis=2)
    gout = rows(mix_out_norm)
    wout, wffn, wfo = w_out.astype(BF16), w_ffn_in.astype(BF16), w_ffn_out.astype(BF16)
    wple, wgate = w_ple.astype(BF16), w_ple_gate.astype(BF16)
    p = p.reshape(depth, B * S, PLE_DIM)

    h = x
    for i in range(depth):
        qt, xqt, k, xk, vt, gm = _mix_in(
            i, h, rows(mix_pre_norm), wqv, wk, wf, bf, wg, tril, placeq, placek, constq, constk,
            rows(gm_v_norm), ws, bs, gout)
        att = _attn(qt, xqt, k, xk, vt)
        h = _post(
            i, h.reshape(B * S, D), att.reshape(B * S, D_ATT), gm.reshape(B * S, D_GM), p, gout,
            wout, rows(mix_post_norm), rows(ffn_pre_norm), wffn, wfo, rows(ffn_post_norm),
            wple, rows(ple_norm), wgate,
        ).reshape(B, S, D)
    return h
```

```python
import numpy as np
import jax
import jax.numpy as jnp
from jax import lax
from jax.experimental import pallas as pl
from jax.experimental.pallas import tpu as pltpu

D_MODEL = 1024
HEAD_DIM = 64
N_HEADS = 8
D_ATT = N_HEADS * HEAD_DIM
N_PAIRS = N_HEADS // 2
LANES = 128
N_GROUPS = 8
GROUP_DIM = 64
D_GM = N_GROUPS * GROUP_DIM
CHUNK = 128
D_FF = 2816
PLE_DIM = 256
EPS = 1e-6
NEG_INF = -1e30
LOG2E = 1.4426950408889634
N_SPLIT = 3
N_BIAS = 2 * N_SPLIT
EXT = 2 * LANES

TM_IN = 512
TQ = 1024
KC = 256
N_SLOT = TQ // KC
ONES_ROWS = 16
ACC_ROWS = HEAD_DIM + ONES_ROWS
TM_POST = 512
SUB_POST = 256
VMEM_LIMIT = 56 * 1024 * 1024

F32 = jnp.float32
BF16 = jnp.bfloat16
NT = (((1,), (1,)), ((), ()))
TN = (((0,), (0,)), ((), ()))


def _rms(x):
    return x * lax.rsqrt(jnp.mean(x * x, axis=-1, keepdims=True) + EPS)


def _split3(x):
    hi = x.astype(BF16)
    r1 = x - hi.astype(F32)
    mid = r1.astype(BF16)
    lo = (r1 - mid.astype(F32)).astype(BF16)
    return jnp.concatenate([hi, mid, lo], axis=1)


def _dot(a, b):
    return jnp.dot(a, b, preferred_element_type=F32)


def _dot_nt(a, b):
    return lax.dot_general(a, b, NT, preferred_element_type=F32)


def _mix_in_kernel(h_ref, gain_ref, wqv_ref, wk_ref, wf_ref, bf_ref, wg_ref, tril_ref,
                   placeq_ref, placek_ref, constq_ref, constk_ref, vgain_ref, ws_ref, bs_ref,
                   ggm_ref, qt_ref, xqt_ref, k_ref, xk_ref, vt_ref, gm_ref, carry_ref, gm_scr):
    tm = h_ref.shape[1]

    @pl.when(pl.program_id(1) == 0)
    def _():
        carry_ref[...] = jnp.zeros_like(carry_ref)

    hb = (_rms(h_ref[0]) * gain_ref[...]).astype(BF16)
    ge = jax.nn.gelu(_dot_nt(wg_ref[...], hb))

    zf = _dot(hb, wf_ref[...]) + bf_ref[...]
    zqv = _dot_nt(wqv_ref[...], hb)
    qt_ref[0] = (zqv[:D_ATT] * (HEAD_DIM ** -0.5 * LOG2E)).astype(BF16)
    vt = zqv[D_ATT:].astype(BF16)
    for n in range(tm // KC):
        vt_ref[0, n] = vt[:, n * KC:(n + 1) * KC]

    logf = jnp.minimum(zf, 0.0) - jnp.log1p(jnp.exp(-jnp.abs(zf)))
    lane = lax.broadcasted_iota(jnp.int32, (1, LANES), 1)
    logf = jnp.where(lane < N_HEADS, logf, 0.0)
    cs = _dot(tril_ref[...], _split3(logf))
    c = cs[:, :LANES] + cs[:, LANES:2 * LANES] + cs[:, 2 * LANES:] + carry_ref[...]
    carry_ref[...] = c[tm - 1:tm, :]

    n_chunks = tm // CHUNK
    row = lax.broadcasted_iota(jnp.int32, (CHUNK, CHUNK), 0)
    col = lax.broadcasted_iota(jnp.int32, (CHUNK, CHUNK), 1)
    for g in range(N_GROUPS):
        grp = slice(g * GROUP_DIM, (g + 1) * GROUP_DIM)
        x = ge[D_GM + g * GROUP_DIM:D_GM + (g + 1) * GROUP_DIM]
        d = x - jnp.mean(x, axis=0, keepdims=True)
        var = jnp.mean(d * d, axis=0, keepdims=True)
        vn = (d * lax.rsqrt(var + EPS) * vgain_ref[grp]).astype(BF16)
        w = jnp.where(row >= col, ws_ref[g], 0.0).astype(BF16)
        stacked = jnp.concatenate([vn[:, n * CHUNK:(n + 1) * CHUNK] for n in range(n_chunks)], axis=0)
        mm = _dot_nt(stacked, w)
        mixed = jnp.concatenate([mm[n * GROUP_DIM:(n + 1) * GROUP_DIM] for n in range(n_chunks)], axis=1)
        bias = jnp.concatenate([bs_ref[g:g + 1]] * n_chunks, axis=1)
        gm_scr[grp] = ge[grp] * (mixed + bias)

    k_ref[0] = _dot(hb, wk_ref[...]).astype(BF16)
    c3 = _split3(c * LOG2E)
    xqt_ref[0] = (_dot_nt(placeq_ref[...], c3) + constq_ref[...]).astype(BF16)
    xk_ref[0] = (_dot(c3, placek_ref[...]) + constk_ref[...]).astype(BF16)

    gm = gm_scr[...]
    scale = lax.rsqrt(jnp.mean(gm * gm, axis=0, keepdims=True) + EPS)
    gm_ref[0] = (gm * scale * ggm_ref[...]).astype(BF16)


def _mix_in(layer, h, gain, wqv, wk, wf, bf, wg, tril, placeq, placek, constq, constk, vgain, ws, bs, gout):
    B, S, _ = h.shape
    tm = TM_IN
    const = lambda shape: pl.BlockSpec(shape, lambda b, s: (0,) * len(shape))
    param = lambda *shape: pl.BlockSpec((None,) + shape, lambda b, s: (layer,) + (0,) * len(shape))
    rows = lambda w: pl.BlockSpec((1, tm, w), lambda b, s: (b, s, 0))
    cols = lambda w: pl.BlockSpec((1, w, tm), lambda b, s: (b, 0, s))
    return pl.pallas_call(
        _mix_in_kernel,
        grid=(B, S // tm),
        in_specs=[
            rows(D_MODEL), param(1, D_MODEL), param(2 * D_ATT, D_MODEL),
            param(D_MODEL, D_ATT), param(D_MODEL, LANES), param(1, LANES),
            param(2 * D_GM, D_MODEL), const((tm, tm)),
            const((LANES, N_SPLIT * LANES)), const((N_SPLIT * LANES, LANES)),
            const((LANES, 1)), const((1, LANES)), param(D_GM, 1),
            param(N_GROUPS, CHUNK, CHUNK), param(N_GROUPS, CHUNK), param(D_GM, 1),
        ],
        out_specs=[
            cols(D_ATT), cols(LANES), rows(D_ATT), rows(LANES),
            pl.BlockSpec((1, tm // KC, D_ATT, KC), lambda b, s: (b, s, 0, 0)),
            cols(D_GM),
        ],
        out_shape=[
            jax.ShapeDtypeStruct((B, D_ATT, S), BF16),
            jax.ShapeDtypeStruct((B, LANES, S), BF16),
            jax.ShapeDtypeStruct((B, S, D_ATT), BF16),
            jax.ShapeDtypeStruct((B, S, LANES), BF16),
            jax.ShapeDtypeStruct((B, S // KC, D_ATT, KC), BF16),
            jax.ShapeDtypeStruct((B, D_GM, S), BF16),
        ],
        scratch_shapes=[pltpu.VMEM((1, LANES), F32), pltpu.VMEM((D_GM, tm), F32)],
        compiler_params=pltpu.CompilerParams(
            dimension_semantics=("arbitrary", "arbitrary"), vmem_limit_bytes=VMEM_LIMIT),
        name="mix_in",
    )(h, gain, wqv, wk, wf, bf, wg, tril, placeq, placek, constq, constk, vgain, ws, bs, gout)


def _attn_kernel(qt_ref, xqt_ref, k_ref, xk_ref, vt_ref, o_ref, qh_ref, acc_ref, st_ref, p_ref):
    pair = pl.program_id(1)
    row = lax.broadcasted_iota(jnp.int32, (EXT, 1), 0)
    own = []
    for hh in range(2):
        slot = LANES + N_BIAS * (2 * pair + hh)
        own.append(((row >= hh * HEAD_DIM) & (row < (hh + 1) * HEAD_DIM)) | ((row >= slot) & (row < slot + N_BIAS)))
    ones = (lax.broadcasted_iota(jnp.int32, (ONES_ROWS, KC), 0) == 0).astype(BF16)

    @pl.when((pl.program_id(0) == 0) & (pair == 0))
    def _():
        p_ref[2:] = jnp.zeros((2,) + p_ref.shape[1:], BF16)

    for i in range(qt_ref.shape[2] // TQ):
        _attn_tile(i, own, ones, qt_ref, xqt_ref, k_ref, xk_ref, vt_ref, o_ref,
                   qh_ref.at[i % 2], acc_ref.at[i % 2], st_ref, p_ref)


def _attn_tile(i, own, ones, qt_ref, xqt_ref, k_ref, xk_ref, vt_ref, o_ref, qh_ref, acc_ref, st_ref, p_ref):
    tq = TQ
    cols = slice(i * tq, (i + 1) * tq)
    qt = jnp.concatenate([qt_ref[0, :, cols], xqt_ref[0, :, cols]], axis=0)
    for hh in range(2):
        qh_ref[hh] = jnp.where(own[hh], qt, jnp.zeros_like(qt))
    acc_ref[...] = jnp.zeros(acc_ref.shape, F32)

    def scores(c, slot, lo=0):
        start = c * KC if isinstance(c, int) else pl.multiple_of(c * KC, KC)
        k = jnp.concatenate([k_ref[0, pl.ds(start, KC), :], xk_ref[0, pl.ds(start, KC), :]], axis=1)
        for hh in range(2):
            st_ref[slot, hh, :, lo:] = _dot(k, qh_ref[hh, :, lo:])

    def softmax(slot, m, lo=0, diagonal=False):
        m_out, alpha = [], []
        for hh in range(2):
            st = st_ref[slot, hh, :, lo:]
            if diagonal:
                key = lax.broadcasted_iota(jnp.int32, st.shape, 0)
                qry = lax.broadcasted_iota(jnp.int32, st.shape, 1)
                st = jnp.where(key <= qry, st, NEG_INF)
            m_prev = m[hh][:, lo:]
            m_next = jnp.maximum(m_prev, jnp.max(st, axis=0, keepdims=True))
            p_ref[slot, hh, :, lo:] = jnp.exp2(st - m_next).astype(BF16)
            alpha.append(jnp.exp2(m_prev - m_next))
            m_out.append(jnp.concatenate([m[hh][:, :lo], m_next], axis=1) if lo else m_next)
        return tuple(m_out), tuple(alpha)

    def pv(c, slot, alpha, lo=0, live=None):
        vt = vt_ref[0, c]
        for hh in range(2):
            vx = jnp.concatenate([vt[hh * HEAD_DIM:(hh + 1) * HEAD_DIM], ones], axis=0)
            if live is not None:
                vx = jnp.where(live, vx, jnp.zeros_like(vx))
            acc_ref[hh, :, lo:] = alpha[hh] * acc_ref[hh, :, lo:] + _dot(vx, p_ref[slot, hh, :, lo:])

    def body(t, carry):
        m, pending = carry
        for s in (0, 2):
            c = N_SLOT * t + s
            o = (s + 2) % N_SLOT
            scores(c + 2, o)
            scores(c + 3, o + 1)
            m, a0 = softmax(s, m)
            m, a1 = softmax(s + 1, m)
            live = None if s else t > 0
            pv(jnp.maximum(c - 2, 0), o, pending[0], 0, live)
            pv(jnp.maximum(c - 1, 0), o + 1, pending[1], 0, live)
            pending = (a0, a1)
        return m, pending

    assert tq == N_SLOT * KC and N_SLOT == 4
    lowest = jnp.full((1, tq), NEG_INF, F32)
    one = jnp.ones((1, tq), F32)
    scores(0, 0)
    scores(1, 1)
    carry = ((lowest, lowest), ((one, one), (one, one)))
    m, pending = lax.fori_loop(0, i, body, carry) if i else carry
    c = N_SLOT * i
    scores(c + 2, 2, 2 * KC)
    scores(c + 3, 3, 3 * KC)
    m, a0 = softmax(0, m, 0, True)
    m, a1 = softmax(1, m, KC, True)
    if i:
        pv(c - 2, 2, pending[0])
        pv(c - 1, 3, pending[1])
    m, a2 = softmax(2, m, 2 * KC, True)
    m, a3 = softmax(3, m, 3 * KC, True)
    pv(c, 0, a0)
    pv(c + 1, 1, a1, KC)
    pv(c + 2, 2, a2, 2 * KC)
    pv(c + 3, 3, a3, 3 * KC)
    out_t = jnp.concatenate(
        [acc_ref[hh, :HEAD_DIM] / acc_ref[hh, HEAD_DIM:HEAD_DIM + 1] for hh in range(2)], axis=0)
    o_ref[0, cols, :] = out_t.T


def _attn(qt, xqt, k, xk, vt):
    B, S, _ = k.shape
    return pl.pallas_call(
        _attn_kernel,
        grid=(B, N_PAIRS),
        in_specs=[
            pl.BlockSpec((1, LANES, S), lambda b, j: (b, j, 0)),
            pl.BlockSpec((1, LANES, S), lambda b, j: (b, 0, 0)),
            pl.BlockSpec((1, S, LANES), lambda b, j: (b, 0, j)),
            pl.BlockSpec((1, S, LANES), lambda b, j: (b, 0, 0)),
            pl.BlockSpec((1, S // KC, LANES, KC), lambda b, j: (b, 0, j, 0)),
        ],
        out_specs=pl.BlockSpec((1, S, LANES), lambda b, j: (b, 0, j)),
        out_shape=jax.ShapeDtypeStruct((B, S, D_ATT), F32),
        scratch_shapes=[
            pltpu.VMEM((2, 2, EXT, TQ), BF16),
            pltpu.VMEM((2, 2, ACC_ROWS, TQ), F32),
            pltpu.VMEM((N_SLOT, 2, KC, TQ), F32),
            pltpu.VMEM((N_SLOT, 2, KC, TQ), BF16),
        ],
        compiler_params=pltpu.CompilerParams(
            dimension_semantics=("arbitrary", "arbitrary"), vmem_limit_bytes=VMEM_LIMIT),
        name="attn",
    )(qt, xqt, k, xk, vt)


def _post_kernel(h_ref, att_ref, gm_ref, p_ref, gatt_ref, wout_ref, gpost_ref, gpre_ref,
                 wa_ref, wb_ref, wfo_ref, gfpost_ref, wple_ref, gple_ref, wgate_ref, o_ref):
    subs = [slice(r * SUB_POST, (r + 1) * SUB_POST) for r in range(h_ref.shape[0] // SUB_POST)]
    e = [_rms(_dot(p_ref[s].astype(BF16), wple_ref[...])) * gple_ref[...] for s in subs]
    an = [(_rms(att_ref[s]) * gatt_ref[...]).astype(BF16) for s in subs]
    y = [_dot(a, wout_ref[:D_ATT]) + lax.dot_general(gm_ref[:, s], wout_ref[D_ATT:], TN, preferred_element_type=F32)
         for s, a in zip(subs, an)]
    h1 = [h_ref[s] + _rms(t) * gpost_ref[...] for s, t in zip(subs, y)]
    hn = [(_rms(t) * gpre_ref[...]).astype(BF16) for t in h1]
    act = [(jax.nn.silu(_dot(t, wa_ref[...])) * _dot(t, wb_ref[...])).astype(BF16) for t in hn]
    y = [_dot(t, wfo_ref[...]) for t in act]
    h2 = [t + _rms(u) * gfpost_ref[...] for t, u in zip(h1, y)]
    gate = [jax.nn.sigmoid(_dot(_rms(t).astype(BF16), wgate_ref[...])) for t in h2]
    for s, t, g, u in zip(subs, h2, gate, e):
        o_ref[s] = t + g * u


def _post(layer, h, att, gm, p, gout, wout, gpost, gpre, wffn, wfo, gfpost, wple, gple, wgate):
    N = h.shape[0]
    tm = TM_POST
    per_seq = gm.shape[2] // tm
    param = lambda *shape, col=0: pl.BlockSpec(
        (None,) + shape, lambda r: (layer, 0, col), pipeline_mode=pl.Buffered(1))
    tile = lambda w: pl.BlockSpec((tm, w), lambda r: (r, 0))
    return pl.pallas_call(
        _post_kernel,
        grid=(N // tm,),
        in_specs=[
            tile(D_MODEL), tile(D_ATT),
            pl.BlockSpec((None, D_GM, tm), lambda r: (r // per_seq, 0, r % per_seq)),
            pl.BlockSpec((None, tm, PLE_DIM), lambda r: (layer, r, 0)),
            param(1, D_ATT), param(D_MODEL, D_MODEL), param(1, D_MODEL), param(1, D_MODEL),
            param(D_MODEL, D_FF), param(D_MODEL, D_FF, col=1), param(D_FF, D_MODEL),
            param(1, D_MODEL), param(PLE_DIM, D_MODEL), param(1, D_MODEL),
            param(D_MODEL, D_MODEL),
        ],
        out_specs=tile(D_MODEL),
        out_shape=jax.ShapeDtypeStruct((N, D_MODEL), F32),
        compiler_params=pltpu.CompilerParams(
            dimension_semantics=("arbitrary",), vmem_limit_bytes=VMEM_LIMIT),
        name="post",
    )(h, att, gm, p, gout, wout, gpost, gpre, wffn, wffn, wfo, gfpost, wple, gple, wgate)


def _placement():
    placeq = np.zeros((LANES, N_SPLIT * LANES), np.float32)
    placek = np.zeros((N_SPLIT * LANES, LANES), np.float32)
    constq = np.zeros((LANES, 1), np.float32)
    constk = np.zeros((1, LANES), np.float32)
    for h in range(N_HEADS):
        for t in range(N_SPLIT):
            placeq[N_BIAS * h + t, t * LANES + h] = 1.0
            placek[t * LANES + h, N_BIAS * h + N_SPLIT + t] = -1.0
            constq[N_BIAS * h + N_SPLIT + t, 0] = 1.0
            constk[0, N_BIAS * h + t] = 1.0
    return placeq, placek, constq, constk


def kernel(x, p, mix_pre_norm, mix_post_norm, w_in, b_forget, gm_v_norm, gm_w_s, gm_b_s,
           mix_out_norm, w_out, ffn_pre_norm, ffn_post_norm, w_ffn_in, w_ffn_out, w_ple,
           ple_norm, w_ple_gate):
    B, S, D = x.shape
    depth = w_in.shape[0]
    placeq, placek, constq, constk = _placement()
    placeq = jnp.asarray(placeq, BF16)
    placek = jnp.asarray(placek, BF16)
    constq = jnp.asarray(constq)
    constk = jnp.asarray(constk)
    tril = jnp.asarray(np.tril(np.ones((TM_IN, TM_IN), np.float32)), BF16)

    rows = lambda g: g.reshape(depth, 1, -1)
    wqv = jnp.concatenate([w_in[:, :, :D_ATT], w_in[:, :, 2 * D_ATT:3 * D_ATT]], axis=2)
    wqv = wqv.astype(BF16).transpose(0, 2, 1)
    wk = w_in[:, :, D_ATT:2 * D_ATT].astype(BF16)
    wf = jnp.pad(w_in[:, :, 3 * D_ATT:3 * D_ATT + N_HEADS].astype(BF16),
                 ((0, 0), (0, 0), (0, LANES - N_HEADS)))
    wg = w_in[:, :, 3 * D_ATT + N_HEADS:].astype(BF16).transpose(0, 2, 1)
    bf = rows(jnp.pad(b_forget, ((0, 0), (0, LANES - N_HEADS))))
    cols = lambda g: g.reshape(depth, -1, 1)
    gout = rows(mix_out_norm)
    wout, wffn, wfo = w_out.astype(BF16), w_ffn_in.astype(BF16), w_ffn_out.astype(BF16)
    wple, wgate = w_ple.astype(BF16), w_ple_gate.astype(BF16)
    p = p.reshape(depth, B * S, PLE_DIM)

    h = x
    for i in range(depth):
        qt, xqt, k, xk, vt, gm = _mix_in(
            i, h, rows(mix_pre_norm), wqv, wk, wf, bf, wg, tril, placeq, placek, constq, constk,
            cols(gm_v_norm), gm_w_s, gm_b_s, cols(mix_out_norm[:, D_ATT:]))
        att = _attn(qt, xqt, k, xk, vt)
        h = _post(
            i, h.reshape(B * S, D), att.reshape(B * S, D_ATT), gm, p, gout,
            wout, rows(mix_post_norm), rows(ffn_pre_norm), wffn, wfo, rows(ffn_post_norm),
            wple, rows(ple_norm), wgate,
        ).reshape(B, S, D)
    return h
```

```python
import numpy as np
import jax
import jax.numpy as jnp
from jax import lax
from jax.experimental import pallas as pl
from jax.experimental.pallas import tpu as pltpu

D_MODEL = 1024
HEAD_DIM = 64
N_HEADS = 8
D_ATT = N_HEADS * HEAD_DIM
N_PAIRS = N_HEADS // 2
LANES = 128
N_GROUPS = 8
GROUP_DIM = 64
D_GM = N_GROUPS * GROUP_DIM
CHUNK = 128
D_FF = 2816
PLE_DIM = 256
EPS = 1e-6
NEG_INF = -1e30
LOG2E = 1.4426950408889634
N_SPLIT = 3
N_BIAS = 2 * N_SPLIT
EXT = 2 * LANES

TM_IN = 512
TQ = 2048
KC = 256
N_SLOT = 4
ONES_ROWS = 16
ACC_ROWS = HEAD_DIM + ONES_ROWS
TM_POST = 512
SUB_POST = 256
VMEM_LIMIT = 56 * 1024 * 1024

F32 = jnp.float32
BF16 = jnp.bfloat16
NT = (((1,), (1,)), ((), ()))
TN = (((0,), (0,)), ((), ()))


def _rms(x):
    return x * lax.rsqrt(jnp.mean(x * x, axis=-1, keepdims=True) + EPS)


def _split3(x):
    hi = x.astype(BF16)
    r1 = x - hi.astype(F32)
    mid = r1.astype(BF16)
    lo = (r1 - mid.astype(F32)).astype(BF16)
    return jnp.concatenate([hi, mid, lo], axis=1)


def _dot(a, b):
    return jnp.dot(a, b, preferred_element_type=F32)


def _dot_nt(a, b):
    return lax.dot_general(a, b, NT, preferred_element_type=F32)


def _mix_in_kernel(h_ref, gain_ref, wqv_ref, wk_ref, wf_ref, bf_ref, wg_ref, tril_ref,
                   placeq_ref, placek_ref, constq_ref, constk_ref, vgain_ref, ws_ref, bs_ref,
                   ggm_ref, qt_ref, xqt_ref, k_ref, xk_ref, vt_ref, gm_ref, carry_ref, gm_scr):
    tm = h_ref.shape[1]

    @pl.when(pl.program_id(1) == 0)
    def _():
        carry_ref[...] = jnp.zeros_like(carry_ref)

    hb = (_rms(h_ref[0]) * gain_ref[...]).astype(BF16)
    ge = jax.nn.gelu(_dot_nt(wg_ref[...], hb))

    zf = _dot(hb, wf_ref[...]) + bf_ref[...]
    zqv = _dot_nt(wqv_ref[...], hb)
    qt_ref[0] = (zqv[:D_ATT] * (HEAD_DIM ** -0.5 * LOG2E)).astype(BF16)
    vt = zqv[D_ATT:].astype(BF16)
    for n in range(tm // KC):
        vt_ref[0, n] = vt[:, n * KC:(n + 1) * KC]

    logf = jnp.minimum(zf, 0.0) - jnp.log1p(jnp.exp(-jnp.abs(zf)))
    lane = lax.broadcasted_iota(jnp.int32, (1, LANES), 1)
    logf = jnp.where(lane < N_HEADS, logf, 0.0)
    cs = _dot(tril_ref[...], _split3(logf))
    c = cs[:, :LANES] + cs[:, LANES:2 * LANES] + cs[:, 2 * LANES:] + carry_ref[...]
    carry_ref[...] = c[tm - 1:tm, :]

    n_chunks = tm // CHUNK
    row = lax.broadcasted_iota(jnp.int32, (CHUNK, CHUNK), 0)
    col = lax.broadcasted_iota(jnp.int32, (CHUNK, CHUNK), 1)
    for g in range(N_GROUPS):
        grp = slice(g * GROUP_DIM, (g + 1) * GROUP_DIM)
        x = ge[D_GM + g * GROUP_DIM:D_GM + (g + 1) * GROUP_DIM]
        d = x - jnp.mean(x, axis=0, keepdims=True)
        var = jnp.mean(d * d, axis=0, keepdims=True)
        vn = (d * lax.rsqrt(var + EPS) * vgain_ref[grp]).astype(BF16)
        w = jnp.where(row >= col, ws_ref[g], 0.0).astype(BF16)
        stacked = jnp.concatenate([vn[:, n * CHUNK:(n + 1) * CHUNK] for n in range(n_chunks)], axis=0)
        mm = _dot_nt(stacked, w)
        mixed = jnp.concatenate([mm[n * GROUP_DIM:(n + 1) * GROUP_DIM] for n in range(n_chunks)], axis=1)
        bias = jnp.concatenate([bs_ref[g:g + 1]] * n_chunks, axis=1)
        gm_scr[grp] = ge[grp] * (mixed + bias)

    k_ref[0] = _dot(hb, wk_ref[...]).astype(BF16)
    c3 = _split3(c * LOG2E)
    xqt_ref[0] = (_dot_nt(placeq_ref[...], c3) + constq_ref[...]).astype(BF16)
    xk_ref[0] = (_dot(c3, placek_ref[...]) + constk_ref[...]).astype(BF16)

    gm = gm_scr[...]
    scale = lax.rsqrt(jnp.mean(gm * gm, axis=0, keepdims=True) + EPS)
    gm_ref[0] = (gm * scale * ggm_ref[...]).astype(BF16)


def _mix_in(layer, h, gain, wqv, wk, wf, bf, wg, tril, placeq, placek, constq, constk, vgain, ws, bs, gout):
    B, S, _ = h.shape
    tm = TM_IN
    const = lambda shape: pl.BlockSpec(shape, lambda b, s: (0,) * len(shape))
    param = lambda *shape: pl.BlockSpec((None,) + shape, lambda b, s: (layer,) + (0,) * len(shape))
    rows = lambda w: pl.BlockSpec((1, tm, w), lambda b, s: (b, s, 0))
    cols = lambda w: pl.BlockSpec((1, w, tm), lambda b, s: (b, 0, s))
    return pl.pallas_call(
        _mix_in_kernel,
        grid=(B, S // tm),
        in_specs=[
            rows(D_MODEL), param(1, D_MODEL), param(2 * D_ATT, D_MODEL),
            param(D_MODEL, D_ATT), param(D_MODEL, LANES), param(1, LANES),
            param(2 * D_GM, D_MODEL), const((tm, tm)),
            const((LANES, N_SPLIT * LANES)), const((N_SPLIT * LANES, LANES)),
            const((LANES, 1)), const((1, LANES)), param(D_GM, 1),
            param(N_GROUPS, CHUNK, CHUNK), param(N_GROUPS, CHUNK), param(D_GM, 1),
        ],
        out_specs=[
            cols(D_ATT), cols(LANES), rows(D_ATT), rows(LANES),
            pl.BlockSpec((1, tm // KC, D_ATT, KC), lambda b, s: (b, s, 0, 0)),
            cols(D_GM),
        ],
        out_shape=[
            jax.ShapeDtypeStruct((B, D_ATT, S), BF16),
            jax.ShapeDtypeStruct((B, LANES, S), BF16),
            jax.ShapeDtypeStruct((B, S, D_ATT), BF16),
            jax.ShapeDtypeStruct((B, S, LANES), BF16),
            jax.ShapeDtypeStruct((B, S // KC, D_ATT, KC), BF16),
            jax.ShapeDtypeStruct((B, D_GM, S), BF16),
        ],
        scratch_shapes=[pltpu.VMEM((1, LANES), F32), pltpu.VMEM((D_GM, tm), F32)],
        compiler_params=pltpu.CompilerParams(
            dimension_semantics=("arbitrary", "arbitrary"), vmem_limit_bytes=VMEM_LIMIT),
        name="mix_in",
    )(h, gain, wqv, wk, wf, bf, wg, tril, placeq, placek, constq, constk, vgain, ws, bs, gout)


def _attn_kernel(qt_ref, xqt_ref, k_ref, xk_ref, vt_ref, o_ref, qh_ref, acc_ref, st_ref, p_ref):
    pair = pl.program_id(1)
    row = lax.broadcasted_iota(jnp.int32, (EXT, 1), 0)
    own = []
    for hh in range(2):
        slot = LANES + N_BIAS * (2 * pair + hh)
        own.append(((row >= hh * HEAD_DIM) & (row < (hh + 1) * HEAD_DIM)) | ((row >= slot) & (row < slot + N_BIAS)))
    ones = (lax.broadcasted_iota(jnp.int32, (ONES_ROWS, KC), 0) == 0).astype(BF16)
    for i in range(qt_ref.shape[2] // TQ):
        _attn_tile(i, own, ones, qt_ref, xqt_ref, k_ref, xk_ref, vt_ref, o_ref,
                   qh_ref.at[i % 2], acc_ref.at[i % 2], st_ref, p_ref)


def _attn_tile(i, own, ones, qt_ref, xqt_ref, k_ref, xk_ref, vt_ref, o_ref, qh_ref, acc_ref, st_ref, p_ref):
    tq = TQ
    cols = slice(i * tq, (i + 1) * tq)
    qt = jnp.concatenate([qt_ref[0, :, cols], xqt_ref[0, :, cols]], axis=0)
    for hh in range(2):
        qh_ref[hh] = jnp.where(own[hh], qt, jnp.zeros_like(qt))
    acc_ref[...] = jnp.zeros(acc_ref.shape, F32)

    n_diag = tq // KC
    n_chunks = (i + 1) * n_diag
    lo_of = lambda c: max(c - i * n_diag, 0) * KC

    def scores(c):
        lo = lo_of(c)
        k = jnp.concatenate([k_ref[0, c * KC:(c + 1) * KC, :], xk_ref[0, c * KC:(c + 1) * KC, :]], axis=1)
        for hh in range(2):
            st_ref[c % N_SLOT, hh, :, lo:] = _dot(k, qh_ref[hh, :, lo:])

    def softmax(c, m):
        lo = lo_of(c)
        m_out, alpha = [], []
        for hh in range(2):
            st = st_ref[c % N_SLOT, hh, :, lo:]
            if c >= i * n_diag:
                key = lax.broadcasted_iota(jnp.int32, st.shape, 0)
                qry = lax.broadcasted_iota(jnp.int32, st.shape, 1)
                st = jnp.where(key <= qry, st, NEG_INF)
            m_prev = m[hh][:, lo:]
            m_next = jnp.maximum(m_prev, jnp.max(st, axis=0, keepdims=True))
            p_ref[c % N_SLOT, hh, :, lo:] = jnp.exp2(st - m_next).astype(BF16)
            alpha.append(jnp.exp2(m_prev - m_next))
            m_out.append(jnp.concatenate([m[hh][:, :lo], m_next], axis=1) if lo else m_next)
        return tuple(m_out), tuple(alpha)

    def pv(c, alpha):
        lo = lo_of(c)
        vt = vt_ref[0, c]
        for hh in range(2):
            vx = jnp.concatenate([vt[hh * HEAD_DIM:(hh + 1) * HEAD_DIM], ones], axis=0)
            acc_ref[hh, :, lo:] = alpha[hh] * acc_ref[hh, :, lo:] + _dot(vx, p_ref[c % N_SLOT, hh, :, lo:])

    lowest = jnp.full((1, tq), NEG_INF, F32)
    m = (lowest, lowest)
    alpha = {}
    for t in range(-2, n_chunks + 2, 2):
        for c in (t + 2, t + 3):
            if 0 <= c < n_chunks:
                scores(c)
        for c in (t, t + 1):
            if 0 <= c < n_chunks:
                m, alpha[c] = softmax(c, m)
        for c in (t - 2, t - 1):
            if 0 <= c < n_chunks:
                pv(c, alpha.pop(c))
    out_t = jnp.concatenate(
        [acc_ref[hh, :HEAD_DIM] / acc_ref[hh, HEAD_DIM:HEAD_DIM + 1] for hh in range(2)], axis=0)
    o_ref[0, cols, :] = out_t.T


def _attn(qt, xqt, k, xk, vt):
    B, S, _ = k.shape
    return pl.pallas_call(
        _attn_kernel,
        grid=(B, N_PAIRS),
        in_specs=[
            pl.BlockSpec((1, LANES, S), lambda b, j: (b, j, 0)),
            pl.BlockSpec((1, LANES, S), lambda b, j: (b, 0, 0)),
            pl.BlockSpec((1, S, LANES), lambda b, j: (b, 0, j)),
            pl.BlockSpec((1, S, LANES), lambda b, j: (b, 0, 0)),
            pl.BlockSpec((1, S // KC, LANES, KC), lambda b, j: (b, 0, j, 0)),
        ],
        out_specs=pl.BlockSpec((1, S, LANES), lambda b, j: (b, 0, j)),
        out_shape=jax.ShapeDtypeStruct((B, S, D_ATT), F32),
        scratch_shapes=[
            pltpu.VMEM((2, 2, EXT, TQ), BF16),
            pltpu.VMEM((2, 2, ACC_ROWS, TQ), F32),
            pltpu.VMEM((N_SLOT, 2, KC, TQ), F32),
            pltpu.VMEM((N_SLOT, 2, KC, TQ), BF16),
        ],
        compiler_params=pltpu.CompilerParams(
            dimension_semantics=("arbitrary", "arbitrary"), vmem_limit_bytes=VMEM_LIMIT),
        name="attn",
    )(qt, xqt, k, xk, vt)


def _post_kernel(h_ref, att_ref, gm_ref, p_ref, gatt_ref, wout_ref, gpost_ref, gpre_ref,
                 wa_ref, wb_ref, wfo_ref, gfpost_ref, wple_ref, gple_ref, wgate_ref, o_ref):
    subs = [slice(r * SUB_POST, (r + 1) * SUB_POST) for r in range(h_ref.shape[0] // SUB_POST)]
    e = [_rms(_dot(p_ref[s].astype(BF16), wple_ref[...])) * gple_ref[...] for s in subs]
    an = [(_rms(att_ref[s]) * gatt_ref[...]).astype(BF16) for s in subs]
    y = [_dot(a, wout_ref[:D_ATT]) + lax.dot_general(gm_ref[:, s], wout_ref[D_ATT:], TN, preferred_element_type=F32)
         for s, a in zip(subs, an)]
    h1 = [h_ref[s] + _rms(t) * gpost_ref[...] for s, t in zip(subs, y)]
    hn = [(_rms(t) * gpre_ref[...]).astype(BF16) for t in h1]
    act = [(jax.nn.silu(_dot(t, wa_ref[...])) * _dot(t, wb_ref[...])).astype(BF16) for t in hn]
    y = [_dot(t, wfo_ref[...]) for t in act]
    h2 = [t + _rms(u) * gfpost_ref[...] for t, u in zip(h1, y)]
    gate = [jax.nn.sigmoid(_dot(_rms(t).astype(BF16), wgate_ref[...])) for t in h2]
    for s, t, g, u in zip(subs, h2, gate, e):
        o_ref[s] = t + g * u


def _post(layer, h, att, gm, p, gout, wout, gpost, gpre, wffn, wfo, gfpost, wple, gple, wgate):
    N = h.shape[0]
    tm = TM_POST
    per_seq = gm.shape[2] // tm
    param = lambda *shape, col=0: pl.BlockSpec(
        (None,) + shape, lambda r: (layer, 0, col), pipeline_mode=pl.Buffered(1))
    tile = lambda w: pl.BlockSpec((tm, w), lambda r: (r, 0))
    return pl.pallas_call(
        _post_kernel,
        grid=(N // tm,),
        in_specs=[
            tile(D_MODEL), tile(D_ATT),
            pl.BlockSpec((None, D_GM, tm), lambda r: (r // per_seq, 0, r % per_seq)),
            pl.BlockSpec((None, tm, PLE_DIM), lambda r: (layer, r, 0)),
            param(1, D_ATT), param(D_MODEL, D_MODEL), param(1, D_MODEL), param(1, D_MODEL),
            param(D_MODEL, D_FF), param(D_MODEL, D_FF, col=1), param(D_FF, D_MODEL),
            param(1, D_MODEL), param(PLE_DIM, D_MODEL), param(1, D_MODEL),
            param(D_MODEL, D_MODEL),
        ],
        out_specs=tile(D_MODEL),
        out_shape=jax.ShapeDtypeStruct((N, D_MODEL), F32),
        compiler_params=pltpu.CompilerParams(
            dimension_semantics=("arbitrary",), vmem_limit_bytes=VMEM_LIMIT),
        name="post",
    )(h, att, gm, p, gout, wout, gpost, gpre, wffn, wffn, wfo, gfpost, wple, gple, wgate)


def _placement():
    placeq = np.zeros((LANES, N_SPLIT * LANES), np.float32)
    placek = np.zeros((N_SPLIT * LANES, LANES), np.float32)
    constq = np.zeros((LANES, 1), np.float32)
    constk = np.zeros((1, LANES), np.float32)
    for h in range(N_HEADS):
        for t in range(N_SPLIT):
            placeq[N_BIAS * h + t, t * LANES + h] = 1.0
            placek[t * LANES + h, N_BIAS * h + N_SPLIT + t] = -1.0
            constq[N_BIAS * h + N_SPLIT + t, 0] = 1.0
            constk[0, N_BIAS * h + t] = 1.0
    return placeq, placek, constq, constk


def kernel(x, p, mix_pre_norm, mix_post_norm, w_in, b_forget, gm_v_norm, gm_w_s, gm_b_s,
           mix_out_norm, w_out, ffn_pre_norm, ffn_post_norm, w_ffn_in, w_ffn_out, w_ple,
           ple_norm, w_ple_gate):
    B, S, D = x.shape
    depth = w_in.shape[0]
    placeq, placek, constq, constk = _placement()
    placeq = jnp.asarray(placeq, BF16)
    placek = jnp.asarray(placek, BF16)
    constq = jnp.asarray(constq)
    constk = jnp.asarray(constk)
    tril = jnp.asarray(np.tril(np.ones((TM_IN, TM_IN), np.float32)), BF16)

    rows = lambda g: g.reshape(depth, 1, -1)
    wqv = jnp.concatenate([w_in[:, :, :D_ATT], w_in[:, :, 2 * D_ATT:3 * D_ATT]], axis=2)
    wqv = wqv.astype(BF16).transpose(0, 2, 1)
    wk = w_in[:, :, D_ATT:2 * D_ATT].astype(BF16)
    wf = jnp.pad(w_in[:, :, 3 * D_ATT:3 * D_ATT + N_HEADS].astype(BF16),
                 ((0, 0), (0, 0), (0, LANES - N_HEADS)))
    wg = w_in[:, :, 3 * D_ATT + N_HEADS:].astype(BF16).transpose(0, 2, 1)
    bf = rows(jnp.pad(b_forget, ((0, 0), (0, LANES - N_HEADS))))
    cols = lambda g: g.reshape(depth, -1, 1)
    gout = rows(mix_out_norm)
    wout, wffn, wfo = w_out.astype(BF16), w_ffn_in.astype(BF16), w_ffn_out.astype(BF16)
    wple, wgate = w_ple.astype(BF16), w_ple_gate.astype(BF16)
    p = p.reshape(depth, B * S, PLE_DIM)

    h = x
    for i in range(depth):
        qt, xqt, k, xk, vt, gm = _mix_in(
            i, h, rows(mix_pre_norm), wqv, wk, wf, bf, wg, tril, placeq, placek, constq, constk,
            cols(gm_v_norm), gm_w_s, gm_b_s, cols(mix_out_norm[:, D_ATT:]))
        att = _attn(qt, xqt, k, xk, vt)
        h = _post(
            i, h.reshape(B * S, D), att.reshape(B * S, D_ATT), gm, p, gout,
            wout, rows(mix_post_norm), rows(ffn_pre_norm), wffn, wfo, rows(ffn_post_norm),
            wple, rows(ple_norm), wgate,
        ).reshape(B, S, D)
    return h
```

```python
import numpy as np
import jax
import jax.numpy as jnp
from jax import lax
from jax.experimental import pallas as pl
from jax.experimental.pallas import tpu as pltpu

D_MODEL = 1024
HEAD_DIM = 64
N_HEADS = 8
D_ATT = N_HEADS * HEAD_DIM
N_PAIRS = N_HEADS // 2
LANES = 128
N_GROUPS = 8
GROUP_DIM = 64
D_GM = N_GROUPS * GROUP_DIM
CHUNK = 128
D_FF = 2816
PLE_DIM = 256
EPS = 1e-6
NEG_INF = -1e30
LOG2E = 1.4426950408889634
N_SPLIT = 3
N_BIAS = 2 * N_SPLIT
EXT = 2 * LANES

TM_IN = 1024
SUB_IN = 256
TQ = 2048
KC = 256
N_SLOT = 4
ONES_ROWS = 16
ACC_ROWS = HEAD_DIM + ONES_ROWS
TM_POST = 512
SUB_POST = 256
VMEM_LIMIT = 56 * 1024 * 1024

F32 = jnp.float32
BF16 = jnp.bfloat16
NT = (((1,), (1,)), ((), ()))
TN = (((0,), (0,)), ((), ()))


def _rms(x):
    return x * lax.rsqrt(jnp.mean(x * x, axis=-1, keepdims=True) + EPS)


def _split3(x):
    hi = x.astype(BF16)
    r1 = x - hi.astype(F32)
    mid = r1.astype(BF16)
    lo = (r1 - mid.astype(F32)).astype(BF16)
    return jnp.concatenate([hi, mid, lo], axis=1)


def _dot(a, b):
    return jnp.dot(a, b, preferred_element_type=F32)


def _dot_nt(a, b):
    return lax.dot_general(a, b, NT, preferred_element_type=F32)


def _mix_in_kernel(h_ref, gain_ref, wqv_ref, wk_ref, wf_ref, bf_ref, wg_ref, tril_ref,
                   placeq_ref, placek_ref, constq_ref, constk_ref, vgain_ref, ws_ref, bs_ref,
                   ggm_ref, qt_ref, xqt_ref, k_ref, xk_ref, vt_ref, gm_ref, carry_ref, gm_scr):
    @pl.when(pl.program_id(1) == 0)
    def _():
        carry_ref[...] = jnp.zeros_like(carry_ref)

    for r in range(h_ref.shape[1] // SUB_IN):
        _mix_in_rows(r, h_ref, gain_ref, wqv_ref, wk_ref, wf_ref, bf_ref, wg_ref, tril_ref,
                     placeq_ref, placek_ref, constq_ref, constk_ref, vgain_ref, ws_ref, bs_ref,
                     ggm_ref, qt_ref, xqt_ref, k_ref, xk_ref, vt_ref, gm_ref, carry_ref, gm_scr.at[r % 2])


def _mix_in_rows(r, h_ref, gain_ref, wqv_ref, wk_ref, wf_ref, bf_ref, wg_ref, tril_ref,
                 placeq_ref, placek_ref, constq_ref, constk_ref, vgain_ref, ws_ref, bs_ref,
                 ggm_ref, qt_ref, xqt_ref, k_ref, xk_ref, vt_ref, gm_ref, carry_ref, gm_scr):
    tm = SUB_IN
    rows = slice(r * tm, (r + 1) * tm)
    hb = (_rms(h_ref[0, rows]) * gain_ref[...]).astype(BF16)
    ge = jax.nn.gelu(_dot_nt(wg_ref[...], hb))

    zf = _dot(hb, wf_ref[...]) + bf_ref[...]
    zqv = _dot_nt(wqv_ref[...], hb)
    qt_ref[0, :, rows] = (zqv[:D_ATT] * (HEAD_DIM ** -0.5 * LOG2E)).astype(BF16)
    vt = zqv[D_ATT:].astype(BF16)
    for n in range(tm // KC):
        vt_ref[0, r * (tm // KC) + n] = vt[:, n * KC:(n + 1) * KC]

    logf = jnp.minimum(zf, 0.0) - jnp.log1p(jnp.exp(-jnp.abs(zf)))
    lane = lax.broadcasted_iota(jnp.int32, (1, LANES), 1)
    logf = jnp.where(lane < N_HEADS, logf, 0.0)
    cs = _dot(tril_ref[...], _split3(logf))
    c = cs[:, :LANES] + cs[:, LANES:2 * LANES] + cs[:, 2 * LANES:] + carry_ref[...]
    carry_ref[...] = c[tm - 1:tm, :]

    n_chunks = tm // CHUNK
    row = lax.broadcasted_iota(jnp.int32, (CHUNK, CHUNK), 0)
    col = lax.broadcasted_iota(jnp.int32, (CHUNK, CHUNK), 1)
    for g in range(N_GROUPS):
        grp = slice(g * GROUP_DIM, (g + 1) * GROUP_DIM)
        x = ge[D_GM + g * GROUP_DIM:D_GM + (g + 1) * GROUP_DIM]
        d = x - jnp.mean(x, axis=0, keepdims=True)
        var = jnp.mean(d * d, axis=0, keepdims=True)
        vn = (d * lax.rsqrt(var + EPS) * vgain_ref[grp]).astype(BF16)
        w = jnp.where(row >= col, ws_ref[g], 0.0).astype(BF16)
        stacked = jnp.concatenate([vn[:, n * CHUNK:(n + 1) * CHUNK] for n in range(n_chunks)], axis=0)
        mm = _dot_nt(stacked, w)
        mixed = jnp.concatenate([mm[n * GROUP_DIM:(n + 1) * GROUP_DIM] for n in range(n_chunks)], axis=1)
        bias = jnp.concatenate([bs_ref[g:g + 1]] * n_chunks, axis=1)
        gm_scr[grp] = ge[grp] * (mixed + bias)

    k_ref[0, rows] = _dot(hb, wk_ref[...]).astype(BF16)
    c3 = _split3(c * LOG2E)
    xqt_ref[0, :, rows] = (_dot_nt(placeq_ref[...], c3) + constq_ref[...]).astype(BF16)
    xk_ref[0, rows] = (_dot(c3, placek_ref[...]) + constk_ref[...]).astype(BF16)

    gm = gm_scr[...]
    scale = lax.rsqrt(jnp.mean(gm * gm, axis=0, keepdims=True) + EPS)
    gm_ref[0, :, rows] = (gm * scale * ggm_ref[...]).astype(BF16)


def _mix_in(layer, h, gain, wqv, wk, wf, bf, wg, tril, placeq, placek, constq, constk, vgain, ws, bs, gout):
    B, S, _ = h.shape
    tm = TM_IN
    const = lambda shape: pl.BlockSpec(shape, lambda b, s: (0,) * len(shape))
    param = lambda *shape: pl.BlockSpec((None,) + shape, lambda b, s: (layer,) + (0,) * len(shape))
    rows = lambda w: pl.BlockSpec((1, tm, w), lambda b, s: (b, s, 0))
    cols = lambda w: pl.BlockSpec((1, w, tm), lambda b, s: (b, 0, s))
    return pl.pallas_call(
        _mix_in_kernel,
        grid=(B, S // tm),
        in_specs=[
            rows(D_MODEL), param(1, D_MODEL), param(2 * D_ATT, D_MODEL),
            param(D_MODEL, D_ATT), param(D_MODEL, LANES), param(1, LANES),
            param(2 * D_GM, D_MODEL), const((SUB_IN, SUB_IN)),
            const((LANES, N_SPLIT * LANES)), const((N_SPLIT * LANES, LANES)),
            const((LANES, 1)), const((1, LANES)), param(D_GM, 1),
            param(N_GROUPS, CHUNK, CHUNK), param(N_GROUPS, CHUNK), param(D_GM, 1),
        ],
        out_specs=[
            cols(D_ATT), cols(LANES), rows(D_ATT), rows(LANES),
            pl.BlockSpec((1, tm // KC, D_ATT, KC), lambda b, s: (b, s, 0, 0)),
            cols(D_GM),
        ],
        out_shape=[
            jax.ShapeDtypeStruct((B, D_ATT, S), BF16),
            jax.ShapeDtypeStruct((B, LANES, S), BF16),
            jax.ShapeDtypeStruct((B, S, D_ATT), BF16),
            jax.ShapeDtypeStruct((B, S, LANES), BF16),
            jax.ShapeDtypeStruct((B, S // KC, D_ATT, KC), BF16),
            jax.ShapeDtypeStruct((B, D_GM, S), BF16),
        ],
        scratch_shapes=[pltpu.VMEM((1, LANES), F32), pltpu.VMEM((2, D_GM, SUB_IN), F32)],
        compiler_params=pltpu.CompilerParams(
            dimension_semantics=("arbitrary", "arbitrary"), vmem_limit_bytes=VMEM_LIMIT),
        name="mix_in",
    )(h, gain, wqv, wk, wf, bf, wg, tril, placeq, placek, constq, constk, vgain, ws, bs, gout)


def _attn_kernel(qt_ref, xqt_ref, k_ref, xk_ref, vt_ref, o_ref, qh_ref, acc_ref, st_ref, p_ref):
    pair = pl.program_id(1)
    row = lax.broadcasted_iota(jnp.int32, (EXT, 1), 0)
    own = []
    for hh in range(2):
        slot = LANES + N_BIAS * (2 * pair + hh)
        own.append(((row >= hh * HEAD_DIM) & (row < (hh + 1) * HEAD_DIM)) | ((row >= slot) & (row < slot + N_BIAS)))
    ones = (lax.broadcasted_iota(jnp.int32, (ONES_ROWS, KC), 0) == 0).astype(BF16)
    for i in range(qt_ref.shape[2] // TQ):
        _attn_tile(i, own, ones, qt_ref, xqt_ref, k_ref, xk_ref, vt_ref, o_ref,
                   qh_ref.at[i % 2], acc_ref.at[i % 2], st_ref, p_ref)


def _attn_tile(i, own, ones, qt_ref, xqt_ref, k_ref, xk_ref, vt_ref, o_ref, qh_ref, acc_ref, st_ref, p_ref):
    tq = TQ
    cols = slice(i * tq, (i + 1) * tq)
    qt = jnp.concatenate([qt_ref[0, :, cols], xqt_ref[0, :, cols]], axis=0)
    for hh in range(2):
        qh_ref[hh] = jnp.where(own[hh], qt, jnp.zeros_like(qt))
    acc_ref[...] = jnp.zeros(acc_ref.shape, F32)

    n_diag = tq // KC
    n_chunks = (i + 1) * n_diag
    lo_of = lambda c: max(c - i * n_diag, 0) * KC

    def scores(c):
        lo = lo_of(c)
        k = jnp.concatenate([k_ref[0, c * KC:(c + 1) * KC, :], xk_ref[0, c * KC:(c + 1) * KC, :]], axis=1)
        for hh in range(2):
            st_ref[c % N_SLOT, hh, :, lo:] = _dot(k, qh_ref[hh, :, lo:])

    def softmax(c, m):
        lo = lo_of(c)
        m_out, alpha = [], []
        for hh in range(2):
            st = st_ref[c % N_SLOT, hh, :, lo:]
            if c >= i * n_diag:
                key = lax.broadcasted_iota(jnp.int32, st.shape, 0)
                qry = lax.broadcasted_iota(jnp.int32, st.shape, 1)
                st = jnp.where(key <= qry, st, NEG_INF)
            m_prev = m[hh][:, lo:]
            m_next = jnp.maximum(m_prev, jnp.max(st, axis=0, keepdims=True))
            p_ref[c % N_SLOT, hh, :, lo:] = jnp.exp2(st - m_next).astype(BF16)
            alpha.append(jnp.exp2(m_prev - m_next))
            m_out.append(jnp.concatenate([m[hh][:, :lo], m_next], axis=1) if lo else m_next)
        return tuple(m_out), tuple(alpha)

    def pv(c, alpha):
        lo = lo_of(c)
        vt = vt_ref[0, c]
        for hh in range(2):
            vx = jnp.concatenate([vt[hh * HEAD_DIM:(hh + 1) * HEAD_DIM], ones], axis=0)
            acc_ref[hh, :, lo:] = alpha[hh] * acc_ref[hh, :, lo:] + _dot(vx, p_ref[c % N_SLOT, hh, :, lo:])

    lowest = jnp.full((1, tq), NEG_INF, F32)
    m = (lowest, lowest)
    alpha = {}
    for t in range(-2, n_chunks + 2, 2):
        for c in (t + 2, t + 3):
            if 0 <= c < n_chunks:
                scores(c)
        for c in (t, t + 1):
            if 0 <= c < n_chunks:
                m, alpha[c] = softmax(c, m)
        for c in (t - 2, t - 1):
            if 0 <= c < n_chunks:
                pv(c, alpha.pop(c))
    out_t = jnp.concatenate(
        [acc_ref[hh, :HEAD_DIM] / acc_ref[hh, HEAD_DIM:HEAD_DIM + 1] for hh in range(2)], axis=0)
    o_ref[0, cols, :] = out_t.T


def _attn(qt, xqt, k, xk, vt):
    B, S, _ = k.shape
    return pl.pallas_call(
        _attn_kernel,
        grid=(B, N_PAIRS),
        in_specs=[
            pl.BlockSpec((1, LANES, S), lambda b, j: (b, j, 0)),
            pl.BlockSpec((1, LANES, S), lambda b, j: (b, 0, 0)),
            pl.BlockSpec((1, S, LANES), lambda b, j: (b, 0, j)),
            pl.BlockSpec((1, S, LANES), lambda b, j: (b, 0, 0)),
            pl.BlockSpec((1, S // KC, LANES, KC), lambda b, j: (b, 0, j, 0)),
        ],
        out_specs=pl.BlockSpec((1, S, LANES), lambda b, j: (b, 0, j)),
        out_shape=jax.ShapeDtypeStruct((B, S, D_ATT), F32),
        scratch_shapes=[
            pltpu.VMEM((2, 2, EXT, TQ), BF16),
            pltpu.VMEM((2, 2, ACC_ROWS, TQ), F32),
            pltpu.VMEM((N_SLOT, 2, KC, TQ), F32),
            pltpu.VMEM((N_SLOT, 2, KC, TQ), BF16),
        ],
        compiler_params=pltpu.CompilerParams(
            dimension_semantics=("arbitrary", "arbitrary"), vmem_limit_bytes=VMEM_LIMIT),
        name="attn",
    )(qt, xqt, k, xk, vt)


def _post_kernel(h_ref, att_ref, gm_ref, p_ref, gatt_ref, wout_ref, gpost_ref, gpre_ref,
                 wa_ref, wb_ref, wfo_ref, gfpost_ref, wple_ref, gple_ref, wgate_ref, o_ref):
    subs = [slice(r * SUB_POST, (r + 1) * SUB_POST) for r in range(h_ref.shape[0] // SUB_POST)]
    e = [_rms(_dot(p_ref[s].astype(BF16), wple_ref[...])) * gple_ref[...] for s in subs]
    an = [(_rms(att_ref[s]) * gatt_ref[...]).astype(BF16) for s in subs]
    y = [_dot(a, wout_ref[:D_ATT]) + lax.dot_general(gm_ref[:, s], wout_ref[D_ATT:], TN, preferred_element_type=F32)
         for s, a in zip(subs, an)]
    h1 = [h_ref[s] + _rms(t) * gpost_ref[...] for s, t in zip(subs, y)]
    hn = [(_rms(t) * gpre_ref[...]).astype(BF16) for t in h1]
    act = [(jax.nn.silu(_dot(t, wa_ref[...])) * _dot(t, wb_ref[...])).astype(BF16) for t in hn]
    y = [_dot(t, wfo_ref[...]) for t in act]
    h2 = [t + _rms(u) * gfpost_ref[...] for t, u in zip(h1, y)]
    gate = [jax.nn.sigmoid(_dot(_rms(t).astype(BF16), wgate_ref[...])) for t in h2]
    for s, t, g, u in zip(subs, h2, gate, e):
        o_ref[s] = t + g * u


def _post(layer, h, att, gm, p, gout, wout, gpost, gpre, wffn, wfo, gfpost, wple, gple, wgate):
    N = h.shape[0]
    tm = TM_POST
    per_seq = gm.shape[2] // tm
    param = lambda *shape, col=0: pl.BlockSpec(
        (None,) + shape, lambda r: (layer, 0, col), pipeline_mode=pl.Buffered(1))
    tile = lambda w: pl.BlockSpec((tm, w), lambda r: (r, 0))
    return pl.pallas_call(
        _post_kernel,
        grid=(N // tm,),
        in_specs=[
            tile(D_MODEL), tile(D_ATT),
            pl.BlockSpec((None, D_GM, tm), lambda r: (r // per_seq, 0, r % per_seq)),
            pl.BlockSpec((None, tm, PLE_DIM), lambda r: (layer, r, 0)),
            param(1, D_ATT), param(D_MODEL, D_MODEL), param(1, D_MODEL), param(1, D_MODEL),
            param(D_MODEL, D_FF), param(D_MODEL, D_FF, col=1), param(D_FF, D_MODEL),
            param(1, D_MODEL), param(PLE_DIM, D_MODEL), param(1, D_MODEL),
            param(D_MODEL, D_MODEL),
        ],
        out_specs=tile(D_MODEL),
        out_shape=jax.ShapeDtypeStruct((N, D_MODEL), F32),
        compiler_params=pltpu.CompilerParams(
            dimension_semantics=("arbitrary",), vmem_limit_bytes=VMEM_LIMIT),
        name="post",
    )(h, att, gm, p, gout, wout, gpost, gpre, wffn, wffn, wfo, gfpost, wple, gple, wgate)


def _placement():
    placeq = np.zeros((LANES, N_SPLIT * LANES), np.float32)
    placek = np.zeros((N_SPLIT * LANES, LANES), np.float32)
    constq = np.zeros((LANES, 1), np.float32)
    constk = np.zeros((1, LANES), np.float32)
    for h in range(N_HEADS):
        for t in range(N_SPLIT):
            placeq[N_BIAS * h + t, t * LANES + h] = 1.0
            placek[t * LANES + h, N_BIAS * h + N_SPLIT + t] = -1.0
            constq[N_BIAS * h + N_SPLIT + t, 0] = 1.0
            constk[0, N_BIAS * h + t] = 1.0
    return placeq, placek, constq, constk


def kernel(x, p, mix_pre_norm, mix_post_norm, w_in, b_forget, gm_v_norm, gm_w_s, gm_b_s,
           mix_out_norm, w_out, ffn_pre_norm, ffn_post_norm, w_ffn_in, w_ffn_out, w_ple,
           ple_norm, w_ple_gate):
    B, S, D = x.shape
    depth = w_in.shape[0]
    placeq, placek, constq, constk = _placement()
    placeq = jnp.asarray(placeq, BF16)
    placek = jnp.asarray(placek, BF16)
    constq = jnp.asarray(constq)
    constk = jnp.asarray(constk)
    tril = jnp.asarray(np.tril(np.ones((SUB_IN, SUB_IN), np.float32)), BF16)

    rows = lambda g: g.reshape(depth, 1, -1)
    wqv = jnp.concatenate([w_in[:, :, :D_ATT], w_in[:, :, 2 * D_ATT:3 * D_ATT]], axis=2)
    wqv = wqv.astype(BF16).transpose(0, 2, 1)
    wk = w_in[:, :, D_ATT:2 * D_ATT].astype(BF16)
    wf = jnp.pad(w_in[:, :, 3 * D_ATT:3 * D_ATT + N_HEADS].astype(BF16),
                 ((0, 0), (0, 0), (0, LANES - N_HEADS)))
    wg = w_in[:, :, 3 * D_ATT + N_HEADS:].astype(BF16).transpose(0, 2, 1)
    bf = rows(jnp.pad(b_forget, ((0, 0), (0, LANES - N_HEADS))))
    cols = lambda g: g.reshape(depth, -1, 1)
    gout = rows(mix_out_norm)
    wout, wffn, wfo = w_out.astype(BF16), w_ffn_in.astype(BF16), w_ffn_out.astype(BF16)
    wple, wgate = w_ple.astype(BF16), w_ple_gate.astype(BF16)
    p = p.reshape(depth, B * S, PLE_DIM)

    h = x
    for i in range(depth):
        qt, xqt, k, xk, vt, gm = _mix_in(
            i, h, rows(mix_pre_norm), wqv, wk, wf, bf, wg, tril, placeq, placek, constq, constk,
            cols(gm_v_norm), gm_w_s, gm_b_s, cols(mix_out_norm[:, D_ATT:]))
        att = _attn(qt, xqt, k, xk, vt)
        h = _post(
            i, h.reshape(B * S, D), att.reshape(B * S, D_ATT), gm, p, gout,
            wout, rows(mix_post_norm), rows(ffn_pre_norm), wffn, wfo, rows(ffn_post_norm),
            wple, rows(ple_norm), wgate,
        ).reshape(B, S, D)
    return h
```

```python
import numpy as np
import jax
import jax.numpy as jnp
from jax import lax
from jax.experimental import pallas as pl
from jax.experimental.pallas import tpu as pltpu

D_MODEL = 1024
HEAD_DIM = 64
N_HEADS = 8
D_ATT = N_HEADS * HEAD_DIM
N_PAIRS = N_HEADS // 2
LANES = 128
N_GROUPS = 8
GROUP_DIM = 64
D_GM = N_GROUPS * GROUP_DIM
CHUNK = 128
D_FF = 2816
PLE_DIM = 256
EPS = 1e-6
NEG_INF = -1e30
LOG2E = 1.4426950408889634
N_SPLIT = 3
N_BIAS = 2 * N_SPLIT
EXT = 2 * LANES

TM_IN = 1024
SUB_IN = 256
TQ = 2048
KC = 256
N_SLOT = 4
ONES_ROWS = 16
ACC_ROWS = HEAD_DIM + ONES_ROWS
TM_POST = 512
SUB_POST = 256
VMEM_LIMIT = 56 * 1024 * 1024

F32 = jnp.float32
BF16 = jnp.bfloat16
NT = (((1,), (1,)), ((), ()))
TN = (((0,), (0,)), ((), ()))


def _rms(x):
    return x * lax.rsqrt(jnp.mean(x * x, axis=-1, keepdims=True) + EPS)


def _split3(x):
    hi = x.astype(BF16)
    r1 = x - hi.astype(F32)
    mid = r1.astype(BF16)
    lo = (r1 - mid.astype(F32)).astype(BF16)
    return jnp.concatenate([hi, mid, lo], axis=1)


def _dot(a, b):
    return jnp.dot(a, b, preferred_element_type=F32)


def _dot_nt(a, b):
    return lax.dot_general(a, b, NT, preferred_element_type=F32)


def _mix_in_kernel(h_ref, gain_ref, wqv_ref, wk_ref, wf_ref, bf_ref, wg_ref, tril_ref,
                   placeq_ref, placek_ref, constq_ref, constk_ref, vgain_ref, ws_ref, bs_ref,
                   ggm_ref, p_ref, wple_ref, gple_ref, qt_ref, xqt_ref, k_ref, xk_ref, vt_ref, gm_ref, e_ref,
                   carry_ref, gm_scr):
    @pl.when(pl.program_id(1) == 0)
    def _():
        carry_ref[...] = jnp.zeros_like(carry_ref)

    for r in range(h_ref.shape[1] // SUB_IN):
        _mix_in_rows(r, h_ref, gain_ref, wqv_ref, wk_ref, wf_ref, bf_ref, wg_ref, tril_ref,
                     placeq_ref, placek_ref, constq_ref, constk_ref, vgain_ref, ws_ref, bs_ref,
                     ggm_ref, p_ref, wple_ref, gple_ref, qt_ref, xqt_ref, k_ref, xk_ref, vt_ref, gm_ref, e_ref,
                     carry_ref, gm_scr.at[r % 2])


def _mix_in_rows(r, h_ref, gain_ref, wqv_ref, wk_ref, wf_ref, bf_ref, wg_ref, tril_ref,
                 placeq_ref, placek_ref, constq_ref, constk_ref, vgain_ref, ws_ref, bs_ref,
                 ggm_ref, p_ref, wple_ref, gple_ref, qt_ref, xqt_ref, k_ref, xk_ref, vt_ref, gm_ref, e_ref,
                   carry_ref, gm_scr):
    tm = SUB_IN
    rows = slice(r * tm, (r + 1) * tm)
    hb = (_rms(h_ref[0, rows]) * gain_ref[...]).astype(BF16)
    ge = jax.nn.gelu(_dot_nt(wg_ref[...], hb))

    zf = _dot(hb, wf_ref[...]) + bf_ref[...]
    zqv = _dot_nt(wqv_ref[...], hb)
    qt_ref[0, :, rows] = (zqv[:D_ATT] * (HEAD_DIM ** -0.5 * LOG2E)).astype(BF16)
    vt = zqv[D_ATT:].astype(BF16)
    for n in range(tm // KC):
        vt_ref[0, r * (tm // KC) + n] = vt[:, n * KC:(n + 1) * KC]

    logf = jnp.minimum(zf, 0.0) - jnp.log1p(jnp.exp(-jnp.abs(zf)))
    lane = lax.broadcasted_iota(jnp.int32, (1, LANES), 1)
    logf = jnp.where(lane < N_HEADS, logf, 0.0)
    cs = _dot(tril_ref[...], _split3(logf))
    c = cs[:, :LANES] + cs[:, LANES:2 * LANES] + cs[:, 2 * LANES:] + carry_ref[...]
    carry_ref[...] = c[tm - 1:tm, :]

    n_chunks = tm // CHUNK
    row = lax.broadcasted_iota(jnp.int32, (CHUNK, CHUNK), 0)
    col = lax.broadcasted_iota(jnp.int32, (CHUNK, CHUNK), 1)
    for g in range(N_GROUPS):
        grp = slice(g * GROUP_DIM, (g + 1) * GROUP_DIM)
        x = ge[D_GM + g * GROUP_DIM:D_GM + (g + 1) * GROUP_DIM]
        d = x - jnp.mean(x, axis=0, keepdims=True)
        var = jnp.mean(d * d, axis=0, keepdims=True)
        vn = (d * lax.rsqrt(var + EPS) * vgain_ref[grp]).astype(BF16)
        w = jnp.where(row >= col, ws_ref[g], 0.0).astype(BF16)
        stacked = jnp.concatenate([vn[:, n * CHUNK:(n + 1) * CHUNK] for n in range(n_chunks)], axis=0)
        mm = _dot_nt(stacked, w)
        mixed = jnp.concatenate([mm[n * GROUP_DIM:(n + 1) * GROUP_DIM] for n in range(n_chunks)], axis=1)
        bias = jnp.concatenate([bs_ref[g:g + 1]] * n_chunks, axis=1)
        gm_scr[grp] = ge[grp] * (mixed + bias)

    k_ref[0, rows] = _dot(hb, wk_ref[...]).astype(BF16)
    e_ref[0, rows] = _rms(_dot(p_ref[rows].astype(BF16), wple_ref[...])) * gple_ref[...]
    c3 = _split3(c * LOG2E)
    xqt_ref[0, :, rows] = (_dot_nt(placeq_ref[...], c3) + constq_ref[...]).astype(BF16)
    xk_ref[0, rows] = (_dot(c3, placek_ref[...]) + constk_ref[...]).astype(BF16)

    gm = gm_scr[...]
    scale = lax.rsqrt(jnp.mean(gm * gm, axis=0, keepdims=True) + EPS)
    gm_ref[0, :, rows] = (gm * scale * ggm_ref[...]).astype(BF16)


def _mix_in(layer, h, gain, wqv, wk, wf, bf, wg, tril, placeq, placek, constq, constk, vgain, ws, bs, gout,
            p, wple, gple):
    B, S, _ = h.shape
    tm = TM_IN
    const = lambda shape: pl.BlockSpec(shape, lambda b, s: (0,) * len(shape))
    param = lambda *shape: pl.BlockSpec((None,) + shape, lambda b, s: (layer,) + (0,) * len(shape))
    rows = lambda w: pl.BlockSpec((1, tm, w), lambda b, s: (b, s, 0))
    cols = lambda w: pl.BlockSpec((1, w, tm), lambda b, s: (b, 0, s))
    return pl.pallas_call(
        _mix_in_kernel,
        grid=(B, S // tm),
        in_specs=[
            rows(D_MODEL), param(1, D_MODEL), param(2 * D_ATT, D_MODEL),
            param(D_MODEL, D_ATT), param(D_MODEL, LANES), param(1, LANES),
            param(2 * D_GM, D_MODEL), const((SUB_IN, SUB_IN)),
            const((LANES, N_SPLIT * LANES)), const((N_SPLIT * LANES, LANES)),
            const((LANES, 1)), const((1, LANES)), param(D_GM, 1),
            param(N_GROUPS, CHUNK, CHUNK), param(N_GROUPS, CHUNK), param(D_GM, 1),
            pl.BlockSpec((None, tm, PLE_DIM), lambda b, s: (layer, b * (S // tm) + s, 0)),
            param(PLE_DIM, D_MODEL), param(1, D_MODEL),
        ],
        out_specs=[
            cols(D_ATT), cols(LANES), rows(D_ATT), rows(LANES),
            pl.BlockSpec((1, tm // KC, D_ATT, KC), lambda b, s: (b, s, 0, 0)),
            cols(D_GM), rows(D_MODEL),
        ],
        out_shape=[
            jax.ShapeDtypeStruct((B, D_ATT, S), BF16),
            jax.ShapeDtypeStruct((B, LANES, S), BF16),
            jax.ShapeDtypeStruct((B, S, D_ATT), BF16),
            jax.ShapeDtypeStruct((B, S, LANES), BF16),
            jax.ShapeDtypeStruct((B, S // KC, D_ATT, KC), BF16),
            jax.ShapeDtypeStruct((B, D_GM, S), BF16),
            jax.ShapeDtypeStruct((B, S, D_MODEL), F32),
        ],
        scratch_shapes=[pltpu.VMEM((1, LANES), F32), pltpu.VMEM((2, D_GM, SUB_IN), F32)],
        compiler_params=pltpu.CompilerParams(
            dimension_semantics=("arbitrary", "arbitrary"), vmem_limit_bytes=VMEM_LIMIT),
        name="mix_in",
    )(h, gain, wqv, wk, wf, bf, wg, tril, placeq, placek, constq, constk, vgain, ws, bs, gout, p, wple, gple)


def _attn_kernel(qt_ref, xqt_ref, k_ref, xk_ref, vt_ref, o_ref, qh_ref, acc_ref, st_ref, p_ref):
    pair = pl.program_id(1)
    row = lax.broadcasted_iota(jnp.int32, (EXT, 1), 0)
    own = []
    for hh in range(2):
        slot = LANES + N_BIAS * (2 * pair + hh)
        own.append(((row >= hh * HEAD_DIM) & (row < (hh + 1) * HEAD_DIM)) | ((row >= slot) & (row < slot + N_BIAS)))
    ones = (lax.broadcasted_iota(jnp.int32, (ONES_ROWS, KC), 0) == 0).astype(BF16)
    for i in range(qt_ref.shape[2] // TQ):
        _attn_tile(i, own, ones, qt_ref, xqt_ref, k_ref, xk_ref, vt_ref, o_ref,
                   qh_ref.at[i % 2], acc_ref.at[i % 2], st_ref, p_ref)


def _attn_tile(i, own, ones, qt_ref, xqt_ref, k_ref, xk_ref, vt_ref, o_ref, qh_ref, acc_ref, st_ref, p_ref):
    tq = TQ
    cols = slice(i * tq, (i + 1) * tq)
    qt = jnp.concatenate([qt_ref[0, :, cols], xqt_ref[0, :, cols]], axis=0)
    for hh in range(2):
        qh_ref[hh] = jnp.where(own[hh], qt, jnp.zeros_like(qt))
    acc_ref[...] = jnp.zeros(acc_ref.shape, F32)

    n_diag = tq // KC
    n_chunks = (i + 1) * n_diag
    lo_of = lambda c: max(c - i * n_diag, 0) * KC

    def scores(c):
        lo = lo_of(c)
        k = jnp.concatenate([k_ref[0, c * KC:(c + 1) * KC, :], xk_ref[0, c * KC:(c + 1) * KC, :]], axis=1)
        for hh in range(2):
            st_ref[c % N_SLOT, hh, :, lo:] = _dot(k, qh_ref[hh, :, lo:])

    def softmax(c, m):
        lo = lo_of(c)
        m_out, alpha = [], []
        for hh in range(2):
            st = st_ref[c % N_SLOT, hh, :, lo:]
            if c >= i * n_diag:
                key = lax.broadcasted_iota(jnp.int32, st.shape, 0)
                qry = lax.broadcasted_iota(jnp.int32, st.shape, 1)
                st = jnp.where(key <= qry, st, NEG_INF)
            m_prev = m[hh][:, lo:]
            m_next = jnp.maximum(m_prev, jnp.max(st, axis=0, keepdims=True))
            p_ref[c % N_SLOT, hh, :, lo:] = jnp.exp2(st - m_next).astype(BF16)
            alpha.append(jnp.exp2(m_prev - m_next))
            m_out.append(jnp.concatenate([m[hh][:, :lo], m_next], axis=1) if lo else m_next)
        return tuple(m_out), tuple(alpha)

    def pv(c, alpha):
        lo = lo_of(c)
        vt = vt_ref[0, c]
        for hh in range(2):
            vx = jnp.concatenate([vt[hh * HEAD_DIM:(hh + 1) * HEAD_DIM], ones], axis=0)
            acc_ref[hh, :, lo:] = alpha[hh] * acc_ref[hh, :, lo:] + _dot(vx, p_ref[c % N_SLOT, hh, :, lo:])

    lowest = jnp.full((1, tq), NEG_INF, F32)
    m = (lowest, lowest)
    alpha = {}
    for t in range(-2, n_chunks + 2, 2):
        for c in (t + 2, t + 3):
            if 0 <= c < n_chunks:
                scores(c)
        for c in (t, t + 1):
            if 0 <= c < n_chunks:
                m, alpha[c] = softmax(c, m)
        for c in (t - 2, t - 1):
            if 0 <= c < n_chunks:
                pv(c, alpha.pop(c))
    out_t = jnp.concatenate(
        [acc_ref[hh, :HEAD_DIM] / acc_ref[hh, HEAD_DIM:HEAD_DIM + 1] for hh in range(2)], axis=0)
    o_ref[0, cols, :] = out_t.T


def _attn(qt, xqt, k, xk, vt):
    B, S, _ = k.shape
    return pl.pallas_call(
        _attn_kernel,
        grid=(B, N_PAIRS),
        in_specs=[
            pl.BlockSpec((1, LANES, S), lambda b, j: (b, j, 0)),
            pl.BlockSpec((1, LANES, S), lambda b, j: (b, 0, 0)),
            pl.BlockSpec((1, S, LANES), lambda b, j: (b, 0, j)),
            pl.BlockSpec((1, S, LANES), lambda b, j: (b, 0, 0)),
            pl.BlockSpec((1, S // KC, LANES, KC), lambda b, j: (b, 0, j, 0)),
        ],
        out_specs=pl.BlockSpec((1, S, LANES), lambda b, j: (b, 0, j)),
        out_shape=jax.ShapeDtypeStruct((B, S, D_ATT), F32),
        scratch_shapes=[
            pltpu.VMEM((2, 2, EXT, TQ), BF16),
            pltpu.VMEM((2, 2, ACC_ROWS, TQ), F32),
            pltpu.VMEM((N_SLOT, 2, KC, TQ), F32),
            pltpu.VMEM((N_SLOT, 2, KC, TQ), BF16),
        ],
        compiler_params=pltpu.CompilerParams(
            dimension_semantics=("arbitrary", "arbitrary"), vmem_limit_bytes=VMEM_LIMIT),
        name="attn",
    )(qt, xqt, k, xk, vt)


def _post_kernel(h_ref, att_ref, gm_ref, e_ref, gatt_ref, wout_ref, gpost_ref, gpre_ref,
                 wa_ref, wb_ref, wfo_ref, gfpost_ref, wgate_ref, o_ref):
    subs = [slice(r * SUB_POST, (r + 1) * SUB_POST) for r in range(h_ref.shape[0] // SUB_POST)]
    an = [(_rms(att_ref[s]) * gatt_ref[...]).astype(BF16) for s in subs]
    y = [_dot(a, wout_ref[:D_ATT]) + lax.dot_general(gm_ref[:, s], wout_ref[D_ATT:], TN, preferred_element_type=F32)
         for s, a in zip(subs, an)]
    h1 = [h_ref[s] + _rms(t) * gpost_ref[...] for s, t in zip(subs, y)]
    hn = [(_rms(t) * gpre_ref[...]).astype(BF16) for t in h1]
    act = [(jax.nn.silu(_dot(t, wa_ref[...])) * _dot(t, wb_ref[...])).astype(BF16) for t in hn]
    y = [_dot(t, wfo_ref[...]) for t in act]
    h2 = [t + _rms(u) * gfpost_ref[...] for t, u in zip(h1, y)]
    gate = [jax.nn.sigmoid(_dot(_rms(t).astype(BF16), wgate_ref[...])) for t in h2]
    for s, t, g in zip(subs, h2, gate):
        o_ref[s] = t + g * e_ref[s]


def _post(layer, h, att, gm, e, gout, wout, gpost, gpre, wffn, wfo, gfpost, wgate):
    N = h.shape[0]
    tm = TM_POST
    per_seq = gm.shape[2] // tm
    param = lambda *shape, col=0: pl.BlockSpec(
        (None,) + shape, lambda r: (layer, 0, col), pipeline_mode=pl.Buffered(1))
    tile = lambda w: pl.BlockSpec((tm, w), lambda r: (r, 0))
    return pl.pallas_call(
        _post_kernel,
        grid=(N // tm,),
        in_specs=[
            tile(D_MODEL), tile(D_ATT),
            pl.BlockSpec((None, D_GM, tm), lambda r: (r // per_seq, 0, r % per_seq)),
            tile(D_MODEL),
            param(1, D_ATT), param(D_MODEL, D_MODEL), param(1, D_MODEL), param(1, D_MODEL),
            param(D_MODEL, D_FF), param(D_MODEL, D_FF, col=1), param(D_FF, D_MODEL),
            param(1, D_MODEL), param(D_MODEL, D_MODEL),
        ],
        out_specs=tile(D_MODEL),
        out_shape=jax.ShapeDtypeStruct((N, D_MODEL), F32),
        compiler_params=pltpu.CompilerParams(
            dimension_semantics=("arbitrary",), vmem_limit_bytes=VMEM_LIMIT),
        name="post",
    )(h, att, gm, e, gout, wout, gpost, gpre, wffn, wffn, wfo, gfpost, wgate)


def _placement():
    placeq = np.zeros((LANES, N_SPLIT * LANES), np.float32)
    placek = np.zeros((N_SPLIT * LANES, LANES), np.float32)
    constq = np.zeros((LANES, 1), np.float32)
    constk = np.zeros((1, LANES), np.float32)
    for h in range(N_HEADS):
        for t in range(N_SPLIT):
            placeq[N_BIAS * h + t, t * LANES + h] = 1.0
            placek[t * LANES + h, N_BIAS * h + N_SPLIT + t] = -1.0
            constq[N_BIAS * h + N_SPLIT + t, 0] = 1.0
            constk[0, N_BIAS * h + t] = 1.0
    return placeq, placek, constq, constk


def kernel(x, p, mix_pre_norm, mix_post_norm, w_in, b_forget, gm_v_norm, gm_w_s, gm_b_s,
           mix_out_norm, w_out, ffn_pre_norm, ffn_post_norm, w_ffn_in, w_ffn_out, w_ple,
           ple_norm, w_ple_gate):
    B, S, D = x.shape
    depth = w_in.shape[0]
    placeq, placek, constq, constk = _placement()
    placeq = jnp.asarray(placeq, BF16)
    placek = jnp.asarray(placek, BF16)
    constq = jnp.asarray(constq)
    constk = jnp.asarray(constk)
    tril = jnp.asarray(np.tril(np.ones((SUB_IN, SUB_IN), np.float32)), BF16)

    rows = lambda g: g.reshape(depth, 1, -1)
    wqv = jnp.concatenate([w_in[:, :, :D_ATT], w_in[:, :, 2 * D_ATT:3 * D_ATT]], axis=2)
    wqv = wqv.astype(BF16).transpose(0, 2, 1)
    wk = w_in[:, :, D_ATT:2 * D_ATT].astype(BF16)
    wf = jnp.pad(w_in[:, :, 3 * D_ATT:3 * D_ATT + N_HEADS].astype(BF16),
                 ((0, 0), (0, 0), (0, LANES - N_HEADS)))
    wg = w_in[:, :, 3 * D_ATT + N_HEADS:].astype(BF16).transpose(0, 2, 1)
    bf = rows(jnp.pad(b_forget, ((0, 0), (0, LANES - N_HEADS))))
    cols = lambda g: g.reshape(depth, -1, 1)
    gout = rows(mix_out_norm)
    wout, wffn, wfo = w_out.astype(BF16), w_ffn_in.astype(BF16), w_ffn_out.astype(BF16)
    wple, wgate = w_ple.astype(BF16), w_ple_gate.astype(BF16)
    p = p.reshape(depth, B * S, PLE_DIM)

    h = x
    for i in range(depth):
        qt, xqt, k, xk, vt, gm, e = _mix_in(
            i, h, rows(mix_pre_norm), wqv, wk, wf, bf, wg, tril, placeq, placek, constq, constk,
            cols(gm_v_norm), gm_w_s, gm_b_s, cols(mix_out_norm[:, D_ATT:]), p, wple, rows(ple_norm))
        att = _attn(qt, xqt, k, xk, vt)
        h = _post(
            i, h.reshape(B * S, D), att.reshape(B * S, D_ATT), gm, e.reshape(B * S, D), gout,
            wout, rows(mix_post_norm), rows(ffn_pre_norm), wffn, wfo, rows(ffn_post_norm), wgate,
        ).reshape(B, S, D)
    return h
```
